```python
import math
import jax, jax.numpy as jnp
from jax import lax
import numpy as np

D_MODEL = 2048
BATCH = 2
SEQ = 16384
DEPTH = 1

N_META = 16
SSM_WIDTH = 512
SSM_GROUP = 16
SSM_GROUPS = SSM_WIDTH // SSM_GROUP
SSM_STATE = 64
CONV_WIDTH = 1024
CONV_K = 31
D_FF = 5632
FFN_K = 3
NORM_EPS = 1e-6
LN_EPS = 1e-5

kernel_name = "hybrid_s5_conformer_convffn_block"


def rms_norm(x, g):
    xf = x.astype(jnp.float32)
    y = xf * lax.rsqrt(jnp.mean(xf * xf, axis=-1, keepdims=True) + NORM_EPS)
    return (y * g.astype(jnp.float32)).astype(x.dtype)


def layer_norm(x, g, b):
    xf = x.astype(jnp.float32)
    mu = jnp.mean(xf, axis=-1, keepdims=True)
    var = jnp.mean(jnp.square(xf - mu), axis=-1, keepdims=True)
    y = (xf - mu) * lax.rsqrt(var + LN_EPS)
    return (y * g.astype(jnp.float32) + b.astype(jnp.float32)).astype(x.dtype)


def causal_depthwise_conv(x, w, b):
    k_width = w.shape[0]
    seq_len = x.shape[1]
    xp = jnp.pad(x, ((0, 0), (k_width - 1, 0), (0, 0)))
    y = b + xp[:, 0:seq_len] * w[0]
    for k in range(1, k_width):
        y = y + xp[:, k:k + seq_len] * w[k]
    return y


def _complex_linear_combine(e1, e2):
    a1r, a1i, b1r, b1i = e1
    a2r, a2i, b2r, b2i = e2
    ar = a2r * a1r - a2i * a1i
    ai = a2r * a1i + a2i * a1r
    br = a2r * b1r - a2i * b1i + b2r
    bi = a2r * b1i + a2i * b1r + b2i
    return (ar, ai, br, bi)


def s5_ssm(u, lam_re, lam_im, log_step, b_re, b_im, c_re, c_im, d_skip):
    bsz, seq_len, width = u.shape
    f32 = jnp.float32
    uf = u.astype(f32).reshape(bsz, seq_len, SSM_GROUPS, SSM_GROUP)
    lr = lam_re.astype(f32)
    li = lam_im.astype(f32)
    step = jnp.exp(log_step.astype(f32))[:, None]
    mag = jnp.exp(lr * step)
    ar = mag * jnp.cos(li * step)
    ai = mag * jnp.sin(li * step)
    den = lr * lr + li * li
    cr = ((ar - 1.0) * lr + ai * li) / den
    ci = (ai * lr - (ar - 1.0) * li) / den
    br_ = b_re.astype(f32)
    bi_ = b_im.astype(f32)
    bbar_re = cr[..., None] * br_ - ci[..., None] * bi_
    bbar_im = cr[..., None] * bi_ + ci[..., None] * br_
    bu_re = jnp.einsum('blgk,gpk->blgp', uf, bbar_re)
    bu_im = jnp.einsum('blgk,gpk->blgp', uf, bbar_im)
    a_re = jnp.broadcast_to(ar, bu_re.shape)
    a_im = jnp.broadcast_to(ai, bu_im.shape)
    _, _, xs_re, xs_im = lax.associative_scan(
        _complex_linear_combine, (a_re, a_im, bu_re, bu_im), axis=1)
    y = (jnp.einsum('blgp,gkp->blgk', xs_re, c_re.astype(f32))
         - jnp.einsum('blgp,gkp->blgk', xs_im, c_im.astype(f32))
         + d_skip.astype(f32) * uf)
    return y.reshape(bsz, seq_len, width)


def setup_inputs(seed: int = 0) -> dict:
    key = jax.random.key(seed)
    ks = jax.random.split(key, 32)
    f32 = jnp.float32
    nrm = lambda k, shape, scale: jax.random.normal(k, shape, f32) * scale
    gain = lambda k, shape: 1.0 + 0.01 * jax.random.normal(k, shape, f32)
    in_cols = SSM_WIDTH + 2 * CONV_WIDTH + 2 * D_MODEL
    n_idx = jnp.arange(SSM_STATE, dtype=f32)
    lam_re = -0.5 + 0.01 * jax.random.normal(ks[3], (DEPTH, SSM_GROUPS, SSM_STATE), f32)
    lam_im = math.pi * n_idx + 0.01 * jax.random.normal(ks[4], (DEPTH, SSM_GROUPS, SSM_STATE), f32)
    log_step = jax.random.uniform(ks[5], (DEPTH, SSM_GROUPS), f32,
                                  minval=math.log(1e-3), maxval=math.log(1e-1))
    return {
        "x": jax.random.normal(ks[0], (BATCH, SEQ, D_MODEL), f32),
        "meta_tokens": nrm(ks[1], (N_META, D_MODEL), 1.0),
        "norm_mix_pre": gain(ks[2], (DEPTH, D_MODEL)),
        "w_in": nrm(ks[6], (DEPTH, D_MODEL, in_cols), D_MODEL ** -0.5),
        "lam_re": lam_re,
        "lam_im": lam_im,
        "log_step": log_step,
        "ssm_b_re": nrm(ks[7], (DEPTH, SSM_GROUPS, SSM_STATE, SSM_GROUP), (2 * SSM_GROUP) ** -0.5),
        "ssm_b_im": nrm(ks[8], (DEPTH, SSM_GROUPS, SSM_STATE, SSM_GROUP), (2 * SSM_GROUP) ** -0.5),
        "ssm_c_re": nrm(ks[9], (DEPTH, SSM_GROUPS, SSM_GROUP, SSM_STATE), (2 * SSM_STATE) ** -0.5),
        "ssm_c_im": nrm(ks[10], (DEPTH, SSM_GROUPS, SSM_GROUP, SSM_STATE), (2 * SSM_STATE) ** -0.5),
        "ssm_d": nrm(ks[11], (DEPTH, SSM_GROUPS, SSM_GROUP), 1.0),
        "w_ssm_glu": nrm(ks[12], (DEPTH, SSM_WIDTH, SSM_WIDTH), SSM_WIDTH ** -0.5),
        "w_ssm_proj": nrm(ks[13], (DEPTH, SSM_WIDTH, D_MODEL), SSM_WIDTH ** -0.5),
        "conv_dw_w": nrm(ks[14], (DEPTH, CONV_K, CONV_WIDTH), CONV_K ** -0.5),
        "conv_dw_b": nrm(ks[15], (DEPTH, CONV_WIDTH), 0.01),
        "conv_ln_g": gain(ks[16], (DEPTH, CONV_WIDTH)),
        "conv_ln_b": nrm(ks[17], (DEPTH, CONV_WIDTH), 0.01),
        "w_conv_proj": nrm(ks[18], (DEPTH, CONV_WIDTH, D_MODEL), CONV_WIDTH ** -0.5),
        "w_mix_out": nrm(ks[19], (DEPTH, D_MODEL, D_MODEL), D_MODEL ** -0.5),
        "norm_mix_post": gain(ks[20], (DEPTH, D_MODEL)),
        "norm_ffn_pre": gain(ks[21], (DEPTH, D_MODEL)),
        "w_ffn_up": nrm(ks[22], (DEPTH, D_MODEL, 2 * D_FF), D_MODEL ** -0.5),
        "ffn_dw_w": nrm(ks[23], (DEPTH, FFN_K, 2 * D_FF), FFN_K ** -0.5),
        "ffn_dw_b": nrm(ks[24], (DEPTH, 2 * D_FF), 0.01),
        "w_ffn_down": nrm(ks[25], (DEPTH, D_FF, D_MODEL), D_FF ** -0.5),
        "norm_ffn_post": gain(ks[26], (DEPTH, D_MODEL)),
    }


def reference(x, meta_tokens, norm_mix_pre, w_in, lam_re, lam_im, log_step,
              ssm_b_re, ssm_b_im, ssm_c_re, ssm_c_im, ssm_d, w_ssm_glu, w_ssm_proj,
              conv_dw_w, conv_dw_b, conv_ln_g, conv_ln_b, w_conv_proj, w_mix_out,
              norm_mix_post, norm_ffn_pre, w_ffn_up, ffn_dw_w, ffn_dw_b, w_ffn_down,
              norm_ffn_post):
    bsz = x.shape[0]
    meta = jnp.broadcast_to(meta_tokens.astype(x.dtype)[None], (bsz, N_META, D_MODEL))
    h = jnp.concatenate([meta, x], axis=1)
    split_pts = [SSM_WIDTH, SSM_WIDTH + CONV_WIDTH, SSM_WIDTH + 2 * CONV_WIDTH,
                 SSM_WIDTH + 2 * CONV_WIDTH + D_MODEL]
    for i in range(DEPTH):
        n = rms_norm(h, norm_mix_pre[i])
        z = n @ w_in[i]
        u_ssm, conv_val, conv_gate, gate_ssm, gate_conv = jnp.split(z, split_pts, axis=-1)

        y = s5_ssm(u_ssm, lam_re[i], lam_im[i], log_step[i], ssm_b_re[i], ssm_b_im[i],
                   ssm_c_re[i], ssm_c_im[i], ssm_d[i]).astype(x.dtype)
        y = jax.nn.gelu(y)
        y = y * jax.nn.sigmoid(y @ w_ssm_glu[i])
        branch_a = y @ w_ssm_proj[i]

        c = conv_val * jax.nn.sigmoid(conv_gate)
        c = causal_depthwise_conv(c, conv_dw_w[i], conv_dw_b[i])
        c = jax.nn.silu(layer_norm(c, conv_ln_g[i], conv_ln_b[i]))
        branch_b = c @ w_conv_proj[i]

        merged = jax.nn.sigmoid(gate_ssm) * branch_a + jax.nn.sigmoid(gate_conv) * branch_b
        h = h + rms_norm(merged @ w_mix_out[i], norm_mix_post[i])

        n = rms_norm(h, norm_ffn_pre[i])
        up = causal_depthwise_conv(n @ w_ffn_up[i], ffn_dw_w[i], ffn_dw_b[i])
        f_gate, f_val = jnp.split(up, 2, axis=-1)
        f = (jax.nn.gelu(f_gate) * f_val) @ w_ffn_down[i]
        h = h + rms_norm(f, norm_ffn_post[i])
    return h[:, N_META:]
```

```python
import functools

import jax
import jax.numpy as jnp
from jax import lax
from jax.experimental import pallas as pl
from jax.experimental.pallas import tpu as pltpu

F32 = jnp.float32
BF16 = jnp.bfloat16

D_MODEL = 2048
N_META = 16
SSM_WIDTH = 512
SSM_GROUP = 16
SSM_GROUPS = 32
SSM_STATE = 64
N_STATE = SSM_GROUPS * SSM_STATE
CONV_WIDTH = 1024
CONV_K = 31
D_FF = 5632
FFN_K = 3
NORM_EPS = 1e-6
LN_EPS = 1e-5

SSM_Q = 4
CONV_HIST = 32
FFN_HIST = 8
FFN_BLOCK = 512
SUBLANES = 8
CONV_LANES = CONV_WIDTH // SUBLANES

VMEM_LIMIT = 56 * 1024 * 1024


def _params(n_axes):
    return pltpu.CompilerParams(
        dimension_semantics=("arbitrary",) * n_axes,
        vmem_limit_bytes=VMEM_LIMIT)


def _const_spec(shape):
    zeros = (0,) * len(shape)
    return pl.BlockSpec(shape, lambda *_: zeros, pipeline_mode=pl.Buffered(1))


def _rms(x, g):
    ms = jnp.mean(x * x, axis=-1, keepdims=True)
    return x * lax.rsqrt(ms + NORM_EPS) * g


def _dot(a, b):
    return jnp.dot(a, b, preferred_element_type=F32)


def _in_proj_kernel(x_ref, g_ref, w_ref, u_ref, cg_ref):
    n = _rms(x_ref[0], g_ref[...]).astype(BF16)
    u_ref[0] = _dot(n, w_ref[:, 0:SSM_WIDTH])
    blk = 512
    for j in range(CONV_WIDTH // blk):
        v0 = SSM_WIDTH + blk * j
        g0 = SSM_WIDTH + CONV_WIDTH + blk * j
        val = _dot(n, w_ref[:, v0:v0 + blk])
        gate = _dot(n, w_ref[:, g0:g0 + blk])
        cg_ref[0, :, blk * j:blk * (j + 1)] = val * jax.nn.sigmoid(gate)


def _in_proj(x, g, w, tt):
    b, l, _ = x.shape
    ncol = w.shape[1]
    return pl.pallas_call(
        _in_proj_kernel,
        grid=(b, l // tt),
        in_specs=[
            pl.BlockSpec((1, tt, D_MODEL), lambda bi, i: (bi, i, 0)),
            _const_spec((1, D_MODEL)),
            _const_spec((D_MODEL, ncol)),
        ],
        out_specs=[
            pl.BlockSpec((1, tt, SSM_WIDTH), lambda bi, i: (bi, i, 0)),
            pl.BlockSpec((1, tt, CONV_WIDTH), lambda bi, i: (bi, i, 0)),
        ],
        out_shape=[
            jax.ShapeDtypeStruct((b, l, SSM_WIDTH), F32),
            jax.ShapeDtypeStruct((b, l, CONV_WIDTH), F32),
        ],
        compiler_params=_params(2),
    )(x, g, w)


def _ssm_kernel(u_ref, init_ref, tab_ref, wb_ref, wcr_ref, wci_ref, wk_ref, d_ref,
                y_ref, state_out_ref, x_scr, state_scr, *, m, last_row):
    q = SSM_Q
    lane_blk = 128
    tile_n = 256

    @pl.when(pl.program_id(1) == 0)
    def _():
        state_scr[...] = init_ref[...]

    u = u_ref[0]
    ub = u.astype(BF16)

    for n in range(2 * N_STATE // tile_n):
        kb = (n % (N_STATE // tile_n)) // 2
        lhs = jnp.concatenate(
            [ub[:, SSM_WIDTH * s + lane_blk * kb:SSM_WIDTH * s + lane_blk * (kb + 1)]
             for s in range(q)], axis=1)
        x_scr[:, tile_n * n:tile_n * (n + 1)] = _dot(lhs, wb_ref[n])

    w = 512
    row_id = lax.broadcasted_iota(jnp.int32, (SUBLANES, w), 0)
    for j in range(N_STATE // w):
        sr = slice(w * j, w * (j + 1))
        si = slice(N_STATE + w * j, N_STATE + w * (j + 1))
        tabs = [tab_ref[t, :, sr] for t in range(8)]

        def body(rb, carry, sr=sr, si=si, tabs=tabs):
            cr, ci = carry
            rows = pl.ds(pl.multiple_of(rb * SUBLANES, SUBLANES), SUBLANES)
            xr = x_scr[rows, sr]
            xi = x_scr[rows, si]
            for lvl, d in enumerate((1, 2, 4)):
                ar, ai = tabs[2 * lvl], tabs[2 * lvl + 1]
                pr = pltpu.roll(xr, d, 0)
                pi = pltpu.roll(xi, d, 0)
                xr, xi = xr + ar * pr - ai * pi, xi + ar * pi + ai * pr
            ar, ai = tabs[6], tabs[7]
            xr, xi = xr + ar * cr - ai * ci, xi + ar * ci + ai * cr
            x_scr[rows, sr] = jnp.where(row_id == 0, cr, pltpu.roll(xr, 1, 0))
            x_scr[rows, si] = jnp.where(row_id == 0, ci, pltpu.roll(xi, 1, 0))
            state_scr[:, sr] = jnp.broadcast_to(xr[last_row:last_row + 1], (SUBLANES, w))
            state_scr[:, si] = jnp.broadcast_to(xi[last_row:last_row + 1], (SUBLANES, w))
            return (jnp.broadcast_to(xr[SUBLANES - 1:SUBLANES], (SUBLANES, w)),
                    jnp.broadcast_to(xi[SUBLANES - 1:SUBLANES], (SUBLANES, w)))

        lax.fori_loop(0, m // SUBLANES, body, (state_scr[:, sr], state_scr[:, si]))

    state_out_ref[0] = state_scr[...]

    half = 256
    st_per_half = half // SSM_GROUP * SSM_STATE
    for h in range(SSM_WIDTH // half):
        er = x_scr[:, st_per_half * h:st_per_half * (h + 1)].astype(BF16)
        ei = x_scr[:, N_STATE + st_per_half * h:N_STATE + st_per_half * (h + 1)].astype(BF16)
        for r in range(q):
            lanes = slice(SSM_WIDTH * r + half * h, SSM_WIDTH * r + half * (h + 1))
            lhs = jnp.concatenate(
                [ub[:, SSM_WIDTH * s + half * h:SSM_WIDTH * s + half * (h + 1)]
                 for s in range(r + 1)], axis=1)
            acc = _dot(er, wcr_ref[r, h]) + _dot(ei, wci_ref[r, h])
            acc = acc + _dot(lhs, wk_ref[r, h, 0:half * (r + 1), :])
            y = acc + d_ref[:, lanes] * u[:, lanes]
            y_ref[0, :, lanes] = jax.nn.gelu(y)


def _ssm(u4, init_state, prep, m, last_row):
    b, rows, width = u4.shape
    q = SSM_Q
    kern = functools.partial(_ssm_kernel, m=m, last_row=last_row)
    return pl.pallas_call(
        kern,
        grid=(b, rows // m),
        in_specs=[
            pl.BlockSpec((1, m, width), lambda bi, i: (bi, i, 0)),
            _const_spec((SUBLANES, 2 * N_STATE)),
            _const_spec((8, SUBLANES, N_STATE)),
            _const_spec((16, q * 128, 256)),
            _const_spec((q, 2, 1024, 256)),
            _const_spec((q, 2, 1024, 256)),
            _const_spec((q, 2, q * 256, 256)),
            _const_spec((1, width)),
        ],
        out_specs=[
            pl.BlockSpec((1, m, width), lambda bi, i: (bi, i, 0)),
            pl.BlockSpec((1, SUBLANES, 2 * N_STATE), lambda bi, i: (bi, 0, 0)),
        ],
        out_shape=[
            jax.ShapeDtypeStruct((b, rows, width), F32),
            jax.ShapeDtypeStruct((b, SUBLANES, 2 * N_STATE), F32),
        ],
        scratch_shapes=[
            pltpu.VMEM((m, 2 * N_STATE), F32),
            pltpu.VMEM((SUBLANES, 2 * N_STATE), F32),
        ],
        compiler_params=_params(2),
    )(u4, init_state, prep["tab"], prep["wb"], prep["wcr"], prep["wci"], prep["wk"], prep["d"])


def _ssm_prep(lam_re, lam_im, log_step, b_re, b_im, c_re, c_im, d_skip):
    q = SSM_Q
    hp = lax.Precision.HIGHEST
    lr = lam_re.astype(F32)
    li = lam_im.astype(F32)
    step = jnp.exp(log_step.astype(F32))[:, None]
    mag = jnp.exp(lr * step)
    ar = mag * jnp.cos(li * step)
    ai = mag * jnp.sin(li * step)
    den = lr * lr + li * li
    cr = ((ar - 1.0) * lr + ai * li) / den
    ci = (ai * lr - (ar - 1.0) * li) / den
    br_ = b_re.astype(F32)
    bi_ = b_im.astype(F32)
    bbr = cr[..., None] * br_ - ci[..., None] * bi_
    bbi = cr[..., None] * bi_ + ci[..., None] * br_

    def powers(xr, xi, n):
        outr, outi = [jnp.ones_like(xr)], [jnp.zeros_like(xi)]
        for _ in range(n):
            pr, pi = outr[-1], outi[-1]
            outr.append(pr * xr - pi * xi)
            outi.append(pr * xi + pi * xr)
        return outr, outi

    pr, pi = powers(ar, ai, q)

    vre = jnp.stack([pr[q - 1 - s][..., None] * bbr - pi[q - 1 - s][..., None] * bbi for s in range(q)])
    vim = jnp.stack([pr[q - 1 - s][..., None] * bbi + pi[q - 1 - s][..., None] * bbr for s in range(q)])
    val = jnp.stack([vre, vim])
    val = val.reshape(2, q, 8, 4, SSM_STATE, SSM_GROUP)
    nb = jnp.arange(8)[:, None, None]
    gl = jnp.arange(8)[None, :, None]
    gq = jnp.arange(4)[None, None, :]
    mask = (gl == 4 * (nb % 2) + gq).astype(F32)
    wb = jnp.einsum("asnqpc,nlq->anslcqp", val, mask)
    wb = wb.reshape(16, q * 128, 256).astype(BF16)

    cre = c_re.astype(F32)
    cim = c_im.astype(F32)
    care = jnp.stack([cre * pr[t][:, None, :] - cim * pi[t][:, None, :] for t in range(q + 1)])
    caim = jnp.stack([cre * pi[t][:, None, :] + cim * pr[t][:, None, :] for t in range(q + 1)])
    eye = jnp.eye(16, dtype=F32)

    def pack_c(cv):
        cv = cv.reshape(q, 2, 16, SSM_GROUP, SSM_STATE)
        return jnp.einsum("rhgcp,gl->rhgplc", cv, eye).reshape(q, 2, 1024, 256).astype(BF16)

    wcr = pack_c(care[1:])
    wci = pack_c(-caim[1:])

    kt = (jnp.einsum("tgcp,gpd->tgcd", care[:q], bbr, precision=hp)
          - jnp.einsum("tgcp,gpd->tgcd", caim[:q], bbi, precision=hp))
    kt = kt.reshape(q, 2, 16, SSM_GROUP, SSM_GROUP)
    tt = jnp.einsum("thgcd,gl->thgdlc", kt, eye).reshape(q, 2, 256, 256)
    zero = jnp.zeros((2, 256, 256), F32)
    wk = jnp.stack([
        jnp.concatenate([tt[r - s] if s <= r else zero for s in range(q)], axis=1)
        for r in range(q)]).astype(BF16)

    aqr, aqi = powers(pr[q].reshape(1, N_STATE), pi[q].reshape(1, N_STATE), SUBLANES)
    rows = jnp.arange(SUBLANES)[:, None]
    tabs = []
    for d in (1, 2, 4):
        keep = (rows >= d).astype(F32)
        tabs += [keep * aqr[d], keep * aqi[d]]
    tabs += [jnp.concatenate(aqr[1:], axis=0), jnp.concatenate(aqi[1:], axis=0)]
    tab = jnp.stack(tabs)

    d4 = jnp.tile(d_skip.astype(F32).reshape(1, SSM_WIDTH), (1, q))
    return dict(wb=wb, wcr=wcr, wci=wci, wk=wk, tab=tab, d=d4)


def _mix_kernel(x_ref, y_ref, cg_ref, hist_ref, g_ref, wg_ref, wglu_ref, wsp_ref,
                dww_ref, dwb_ref, lng_ref, lnb_ref, wcp_ref,
                merged_ref, hist_out_ref, buf, conv_scr, cb_scr, *, tt):
    hist_rows = CONV_HIST * SUBLANES

    @pl.when(pl.program_id(1) == 0)
    def _():
        buf[0:hist_rows, :] = hist_ref[...]

    @pl.when(pl.program_id(1) != 0)
    def _():
        buf[0:hist_rows, :] = buf[tt * SUBLANES:tt * SUBLANES + hist_rows, :]

    buf[hist_rows:hist_rows + tt * SUBLANES, :] = cg_ref[0].reshape(tt * SUBLANES, CONV_LANES)
    hist_out_ref[0] = buf[tt * SUBLANES:tt * SUBLANES + hist_rows, :]

    tb = 16
    off0 = CONV_HIST - (CONV_K - 1)

    def conv_body(i, _):
        base = i * tb
        accs = [dwb_ref[...] for _ in range(tb)]
        for k in range(CONV_K):
            wk = dww_ref[k]
            for t in range(tb):
                rows = pl.ds(pl.multiple_of((base + t + off0 + k) * SUBLANES, SUBLANES), SUBLANES)
                accs[t] = accs[t] + buf[rows, :] * wk
        for t in range(tb):
            rows = pl.ds(pl.multiple_of((base + t) * SUBLANES, SUBLANES), SUBLANES)
            conv_scr[rows, :] = accs[t]
        return 0

    lax.fori_loop(0, tt // tb, conv_body, 0)

    def ln_body(i, _):
        base = pl.multiple_of(i * tb, tb)
        c = jnp.concatenate(
            [conv_scr[pl.ds(base * SUBLANES + s, tb, stride=SUBLANES), :] for s in range(SUBLANES)],
            axis=1)
        mu = jnp.mean(c, axis=-1, keepdims=True)
        cen = c - mu
        var = jnp.mean(cen * cen, axis=-1, keepdims=True)
        ln = cen * lax.rsqrt(var + LN_EPS) * lng_ref[...] + lnb_ref[...]
        cb_scr[pl.ds(base, tb), :] = (ln * jax.nn.sigmoid(ln)).astype(BF16)
        return 0

    lax.fori_loop(0, tt // tb, ln_body, 0)

    y = y_ref[0]
    yg = (y * jax.nn.sigmoid(_dot(y.astype(BF16), wglu_ref[...]))).astype(BF16)
    cb = cb_scr[...]

    n = _rms(x_ref[0], g_ref[...]).astype(BF16)
    blk = 512
    for j in range(D_MODEL // blk):
        cols = slice(blk * j, blk * (j + 1))
        gcols = slice(D_MODEL + blk * j, D_MODEL + blk * (j + 1))
        ba = _dot(yg, wsp_ref[:, cols])
        bb = _dot(cb, wcp_ref[:, cols])
        gs = jax.nn.sigmoid(_dot(n, wg_ref[:, cols]))
        gc = jax.nn.sigmoid(_dot(n, wg_ref[:, gcols]))
        merged_ref[0, :, cols] = (gs * ba + gc * bb).astype(BF16)


def _mix(x, y, cg, hist, g, wg, wglu, wsp, dww, dwb, lng, lnb, wcp, tt):
    b, l, _ = x.shape
    kern = functools.partial(_mix_kernel, tt=tt)
    tok = lambda width: pl.BlockSpec((1, tt, width), lambda bi, i: (bi, i, 0))
    return pl.pallas_call(
        kern,
        grid=(b, l // tt),
        in_specs=[
            tok(D_MODEL), tok(SSM_WIDTH),
            pl.BlockSpec((1, tt, SUBLANES, CONV_LANES), lambda bi, i: (bi, i, 0, 0)),
            _const_spec((CONV_HIST * SUBLANES, CONV_LANES)),
            _const_spec((1, D_MODEL)),
            _const_spec((D_MODEL, 2 * D_MODEL)),
            _const_spec((SSM_WIDTH, SSM_WIDTH)),
            _const_spec((SSM_WIDTH, D_MODEL)),
            _const_spec((CONV_K, SUBLANES, CONV_LANES)),
            _const_spec((SUBLANES, CONV_LANES)),
            _const_spec((1, CONV_WIDTH)),
            _const_spec((1, CONV_WIDTH)),
            _const_spec((CONV_WIDTH, D_MODEL)),
        ],
        out_specs=[
            tok(D_MODEL),
            pl.BlockSpec((1, CONV_HIST * SUBLANES, CONV_LANES), lambda bi, i: (bi, 0, 0)),
        ],
        out_shape=[
            jax.ShapeDtypeStruct((b, l, D_MODEL), BF16),
            jax.ShapeDtypeStruct((b, CONV_HIST * SUBLANES, CONV_LANES), F32),
        ],
        scratch_shapes=[
            pltpu.VMEM(((CONV_HIST + tt) * SUBLANES, CONV_LANES), F32),
            pltpu.VMEM((tt * SUBLANES, CONV_LANES), F32),
            pltpu.VMEM((tt, CONV_WIDTH), BF16),
        ],
        compiler_params=_params(2),
    )(x, y, cg.reshape(b, l, SUBLANES, CONV_LANES), hist, g, wg, wglu, wsp,
      dww.reshape(CONV_K, SUBLANES, CONV_LANES), dwb.reshape(SUBLANES, CONV_LANES), lng, lnb, wcp)


def _mix_out_kernel(x_ref, m_ref, w_ref, g_ref, h_ref):
    o = _dot(m_ref[0], w_ref[...])
    h_ref[0] = x_ref[0] + _rms(o, g_ref[...])


def _mix_out(x, merged, w, g, tt):
    b, l, _ = x.shape
    tok = pl.BlockSpec((1, tt, D_MODEL), lambda bi, i: (bi, i, 0))
    return pl.pallas_call(
        _mix_out_kernel,
        grid=(b, l // tt),
        in_specs=[tok, tok, _const_spec((D_MODEL, D_MODEL)), _const_spec((1, D_MODEL))],
        out_specs=tok,
        out_shape=jax.ShapeDtypeStruct((b, l, D_MODEL), F32),
        compiler_params=_params(2),
    )(x, merged, w, g)


def _ffn_kernel(h_ref, hist_ref, gpre_ref, wg_ref, wv_ref, dwg_ref, dwv_ref, bg_ref, bv_ref,
                wd_ref, gpost_ref, out_ref, hist_out_ref, n_scr, acc_scr, ubuf, carry, *, tt, nf):
    i = pl.program_id(1)
    j = pl.program_id(2)
    fb = FFN_BLOCK

    @pl.when(j == 0)
    def _():
        n_scr[...] = _rms(h_ref[0], gpre_ref[...]).astype(BF16)

    @pl.when(i == 0)
    def _():
        carry[j] = hist_ref[j]

    n = n_scr[...]
    ubuf[0:FFN_HIST, :] = carry[j]
    ubuf[FFN_HIST:FFN_HIST + tt, 0:fb] = _dot(n, wg_ref[...])
    ubuf[FFN_HIST:FFN_HIST + tt, fb:2 * fb] = _dot(n, wv_ref[...])
    tail = ubuf[tt:tt + FFN_HIST, :]
    carry[j] = tail
    hist_out_ref[0, j] = tail

    def conv(cols, dw_ref, b_ref):
        acc = b_ref[...] + ubuf[FFN_HIST:FFN_HIST + tt, cols] * dw_ref[FFN_K - 1:FFN_K, :]
        for k in range(FFN_K - 1):
            off = FFN_HIST - (FFN_K - 1) + k
            acc = acc + ubuf[off:off + tt, cols] * dw_ref[k:k + 1, :]
        return acc

    fg = conv(slice(0, fb), dwg_ref, bg_ref)
    fv = conv(slice(fb, 2 * fb), dwv_ref, bv_ref)
    act = (jax.nn.gelu(fg) * fv).astype(BF16)
    part = _dot(act, wd_ref[...])

    @pl.when(j == 0)
    def _():
        acc_scr[...] = part

    @pl.when(j != 0)
    def _():
        acc_scr[...] += part

    @pl.when(j == nf - 1)
    def _():
        out_ref[0] = h_ref[0] + _rms(acc_scr[...], gpost_ref[...])


def _ffn(h, hist, gpre, wup, dww, dwb, wdown, gpost, tt):
    b, l, _ = h.shape
    fb = FFN_BLOCK
    nf = D_FF // fb
    kern = functools.partial(_ffn_kernel, tt=tt, nf=nf)
    tok = pl.BlockSpec((1, tt, D_MODEL), lambda bi, i, j: (bi, i, 0))
    return pl.pallas_call(
        kern,
        grid=(b, l // tt, nf),
        in_specs=[
            tok,
            _const_spec((nf, FFN_HIST, 2 * fb)),
            _const_spec((1, D_MODEL)),
            pl.BlockSpec((D_MODEL, fb), lambda bi, i, j: (0, j)),
            pl.BlockSpec((D_MODEL, fb), lambda bi, i, j: (0, nf + j)),
            pl.BlockSpec((FFN_K, fb), lambda bi, i, j: (0, j)),
            pl.BlockSpec((FFN_K, fb), lambda bi, i, j: (0, nf + j)),
            pl.BlockSpec((1, fb), lambda bi, i, j: (0, j)),
            pl.BlockSpec((1, fb), lambda bi, i, j: (0, nf + j)),
            pl.BlockSpec((fb, D_MODEL), lambda bi, i, j: (j, 0)),
            _const_spec((1, D_MODEL)),
        ],
        out_specs=[
            tok,
            pl.BlockSpec((1, nf, FFN_HIST, 2 * fb), lambda bi, i, j: (bi, 0, 0, 0)),
        ],
        out_shape=[
            jax.ShapeDtypeStruct((b, l, D_MODEL), F32),
            jax.ShapeDtypeStruct((b, nf, FFN_HIST, 2 * fb), F32),
        ],
        scratch_shapes=[
            pltpu.VMEM((tt, D_MODEL), BF16),
            pltpu.VMEM((tt, D_MODEL), F32),
            pltpu.VMEM((FFN_HIST + tt, 2 * fb), F32),
            pltpu.VMEM((nf, FFN_HIST, 2 * fb), F32),
        ],
        compiler_params=_params(3),
    )(h, hist, gpre, wup, wup, dww, dww, dwb, dwb, wdown, gpost)


def _block(x, carries, p, ssm, tiles):
    b, l, _ = x.shape
    q = SSM_Q
    tt_in, m_ssm, tt_mix, tt_out, tt_ffn = tiles
    state0, conv_hist0, ffn_hist0 = carries

    u, cg = _in_proj(x, p["g_mix_pre"], p["w_in_a"], tt_in)

    rows = l // q
    u4 = u.reshape(b, rows, q * SSM_WIDTH)
    pad = (-rows) % m_ssm
    if pad:
        u4 = jnp.pad(u4, ((0, 0), (0, pad), (0, 0)))
    last_row = (rows - 1) % SUBLANES
    y4, state = _ssm(u4, state0, ssm, m_ssm, last_row)
    y = y4[:, :rows].reshape(b, l, SSM_WIDTH)

    merged, conv_hist = _mix(x, y, cg, conv_hist0, p["g_mix_pre"], p["w_in_g"], p["w_glu"],
                             p["w_sp"], p["conv_w"], p["conv_b"], p["ln_g"], p["ln_b"],
                             p["w_cp"], tt_mix)
    h1 = _mix_out(x, merged, p["w_mo"], p["g_mix_post"], tt_out)
    out, ffn_hist = _ffn(h1, ffn_hist0, p["g_ffn_pre"], p["w_up"], p["ffn_w"], p["ffn_b"],
                         p["w_down"], p["g_ffn_post"], tt_ffn)
    return out, (state[0], conv_hist[0], ffn_hist[0])


def kernel(x, meta_tokens, norm_mix_pre, w_in, lam_re, lam_im, log_step, ssm_b_re, ssm_b_im,
           ssm_c_re, ssm_c_im, ssm_d, w_ssm_glu, w_ssm_proj, conv_dw_w, conv_dw_b, conv_ln_g,
           conv_ln_b, w_conv_proj, w_mix_out, norm_mix_post, norm_ffn_pre, w_ffn_up, ffn_dw_w,
           ffn_dw_b, w_ffn_down, norm_ffn_post):
    depth = w_in.shape[0]
    n_a = SSM_WIDTH + 2 * CONV_WIDTH
    nf = D_FF // FFN_BLOCK
    zero_carries = (
        jnp.zeros((SUBLANES, 2 * N_STATE), F32),
        jnp.zeros((CONV_HIST * SUBLANES, CONV_LANES), F32),
        jnp.zeros((nf, FFN_HIST, 2 * FFN_BLOCK), F32),
    )
    h_meta = meta_tokens.astype(x.dtype)[None]
    h = x
    row = lambda v: v.astype(F32).reshape(1, -1)
    for i in range(depth):
        p = dict(
            g_mix_pre=row(norm_mix_pre[i]),
            w_in_a=w_in[i][:, :n_a].astype(BF16),
            w_in_g=w_in[i][:, n_a:].astype(BF16),
            w_glu=w_ssm_glu[i].astype(BF16),
            w_sp=w_ssm_proj[i].astype(BF16),
            conv_w=conv_dw_w[i].astype(F32),
            conv_b=row(conv_dw_b[i]),
            ln_g=row(conv_ln_g[i]),
            ln_b=row(conv_ln_b[i]),
            w_cp=w_conv_proj[i].astype(BF16),
            w_mo=w_mix_out[i].astype(BF16),
            g_mix_post=row(norm_mix_post[i]),
            g_ffn_pre=row(norm_ffn_pre[i]),
            w_up=w_ffn_up[i].astype(BF16),
            ffn_w=ffn_dw_w[i].astype(F32),
            ffn_b=row(ffn_dw_b[i]),
            w_down=w_ffn_down[i].astype(BF16),
            g_ffn_post=row(norm_ffn_post[i]),
        )
        ssm = _ssm_prep(lam_re[i], lam_im[i], log_step[i], ssm_b_re[i], ssm_b_im[i],
                        ssm_c_re[i], ssm_c_im[i], ssm_d[i])
        h_meta, carries = _block(h_meta, zero_carries, p, ssm,
                                 (N_META, SUBLANES, N_META, N_META, N_META))
        h, _ = _block(h, carries, p, ssm, (512, 256, 512, 512, 512))
    return h
```

```python
import functools

import jax
import jax.numpy as jnp
from jax import lax
from jax.experimental import pallas as pl
from jax.experimental.pallas import tpu as pltpu

F32 = jnp.float32
BF16 = jnp.bfloat16

D_MODEL = 2048
N_META = 16
SSM_WIDTH = 512
SSM_GROUP = 16
SSM_GROUPS = 32
SSM_STATE = 64
N_STATE = SSM_GROUPS * SSM_STATE
CONV_WIDTH = 1024
CONV_K = 31
D_FF = 5632
FFN_K = 3
NORM_EPS = 1e-6
LN_EPS = 1e-5

SSM_Q = 4
CONV_HIST = 32
FFN_HIST = 8
FFN_BLOCK = 512
FFN_SLAB = 256
FFN_CHUNK = 16
SUBLANES = 8
CONV_LANES = CONV_WIDTH // SUBLANES

VMEM_LIMIT = 56 * 1024 * 1024
MIX_VMEM_LIMIT = 60 * 1024 * 1024


def _params(n_axes, vmem_limit=VMEM_LIMIT):
    return pltpu.CompilerParams(
        dimension_semantics=("arbitrary",) * n_axes,
        vmem_limit_bytes=vmem_limit)


def _const_spec(shape):
    zeros = (0,) * len(shape)
    return pl.BlockSpec(shape, lambda *_: zeros, pipeline_mode=pl.Buffered(1))


def _rms(x, g):
    ms = jnp.mean(x * x, axis=-1, keepdims=True)
    return x * lax.rsqrt(ms + NORM_EPS) * g


def _dot(a, b):
    return jnp.dot(a, b, preferred_element_type=F32)


def _in_proj_kernel(x_ref, g_ref, w_ref, n_ref, u_ref, cg_ref, u_scr, *, tt):
    q = SSM_Q
    lanes = 128
    n = _rms(x_ref[0], g_ref[...]).astype(BF16)
    n_ref[0] = n
    u = _dot(n, w_ref[:, 0:SSM_WIDTH])
    for cb in range(SSM_WIDTH // lanes):
        u_scr[cb] = u[:, lanes * cb:lanes * (cb + 1)]
    for s in range(q):
        for cb in range(SSM_WIDTH // lanes):
            c0 = SSM_WIDTH * s + lanes * cb
            u_ref[0, :, c0:c0 + lanes] = u_scr[cb, pl.ds(s, tt // q, stride=q), :]
    blk = 512
    for j in range(CONV_WIDTH // blk):
        v0 = SSM_WIDTH + blk * j
        g0 = SSM_WIDTH + CONV_WIDTH + blk * j
        val = _dot(n, w_ref[:, v0:v0 + blk])
        gate = _dot(n, w_ref[:, g0:g0 + blk])
        cg = val * jax.nn.sigmoid(gate)
        for e in range(blk // lanes):
            s = j * (blk // lanes) + e
            cg_ref[0, pl.ds(s, tt, stride=SUBLANES), :] = cg[:, lanes * e:lanes * (e + 1)]


def _in_proj(x, g, w, tt):
    b, l, _ = x.shape
    ncol = w.shape[1]
    q = SSM_Q
    return pl.pallas_call(
        functools.partial(_in_proj_kernel, tt=tt),
        grid=(b, l // tt),
        in_specs=[
            pl.BlockSpec((1, tt, D_MODEL), lambda bi, i: (bi, i, 0)),
            _const_spec((1, D_MODEL)),
            _const_spec((D_MODEL, ncol)),
        ],
        out_specs=[
            pl.BlockSpec((1, tt, D_MODEL), lambda bi, i: (bi, i, 0)),
            pl.BlockSpec((1, tt // q, q * SSM_WIDTH), lambda bi, i: (bi, i, 0)),
            pl.BlockSpec((1, tt * SUBLANES, CONV_LANES), lambda bi, i: (bi, i, 0)),
        ],
        out_shape=[
            jax.ShapeDtypeStruct((b, l, D_MODEL), BF16),
            jax.ShapeDtypeStruct((b, l // q, q * SSM_WIDTH), F32),
            jax.ShapeDtypeStruct((b, l * SUBLANES, CONV_LANES), F32),
        ],
        scratch_shapes=[pltpu.VMEM((SSM_WIDTH // 128, tt, 128), F32)],
        compiler_params=_params(2),
    )(x, g, w)


def _ssm_kernel(u_ref, init_ref, tab_ref, wb_ref, wcr_ref, wci_ref, wk_ref, d_ref,
                y_ref, state_out_ref, x_scr, state_scr, *, m, last_row):
    q = SSM_Q
    lane_blk = 128
    tile_n = 256

    @pl.when(pl.program_id(1) == 0)
    def _():
        state_scr[...] = init_ref[...]

    u = u_ref[0]
    ub = u.astype(BF16)

    for n in range(2 * N_STATE // tile_n):
        kb = (n % (N_STATE // tile_n)) // 2
        lhs = jnp.concatenate(
            [ub[:, SSM_WIDTH * s + lane_blk * kb:SSM_WIDTH * s + lane_blk * (kb + 1)]
             for s in range(q)], axis=1)
        x_scr[:, tile_n * n:tile_n * (n + 1)] = _dot(lhs, wb_ref[n])

    w = 512
    row_id = lax.broadcasted_iota(jnp.int32, (SUBLANES, w), 0)
    for j in range(N_STATE // w):
        sr = slice(w * j, w * (j + 1))
        si = slice(N_STATE + w * j, N_STATE + w * (j + 1))
        tabs = [tab_ref[t, :, sr] for t in range(8)]

        def body(rb, carry, sr=sr, si=si, tabs=tabs):
            cr, ci = carry
            rows = pl.ds(pl.multiple_of(rb * SUBLANES, SUBLANES), SUBLANES)
            xr = x_scr[rows, sr]
            xi = x_scr[rows, si]
            for lvl, d in enumerate((1, 2, 4)):
                ar, ai = tabs[2 * lvl], tabs[2 * lvl + 1]
                pr = pltpu.roll(xr, d, 0)
                pi = pltpu.roll(xi, d, 0)
                xr, xi = xr + ar * pr - ai * pi, xi + ar * pi + ai * pr
            ar, ai = tabs[6], tabs[7]
            xr, xi = xr + ar * cr - ai * ci, xi + ar * ci + ai * cr
            x_scr[rows, sr] = jnp.where(row_id == 0, cr, pltpu.roll(xr, 1, 0))
            x_scr[rows, si] = jnp.where(row_id == 0, ci, pltpu.roll(xi, 1, 0))
            state_scr[:, sr] = jnp.broadcast_to(xr[last_row:last_row + 1], (SUBLANES, w))
            state_scr[:, si] = jnp.broadcast_to(xi[last_row:last_row + 1], (SUBLANES, w))
            return (jnp.broadcast_to(xr[SUBLANES - 1:SUBLANES], (SUBLANES, w)),
                    jnp.broadcast_to(xi[SUBLANES - 1:SUBLANES], (SUBLANES, w)))

        lax.fori_loop(0, m // SUBLANES, body, (state_scr[:, sr], state_scr[:, si]))

    state_out_ref[0] = state_scr[...]

    half = 256
    st_per_half = half // SSM_GROUP * SSM_STATE
    for h in range(SSM_WIDTH // half):
        er = x_scr[:, st_per_half * h:st_per_half * (h + 1)].astype(BF16)
        ei = x_scr[:, N_STATE + st_per_half * h:N_STATE + st_per_half * (h + 1)].astype(BF16)
        for r in range(q):
            lanes = slice(SSM_WIDTH * r + half * h, SSM_WIDTH * r + half * (h + 1))
            lhs = jnp.concatenate(
                [ub[:, SSM_WIDTH * s + half * h:SSM_WIDTH * s + half * (h + 1)]
                 for s in range(r + 1)], axis=1)
            acc = _dot(er, wcr_ref[r, h]) + _dot(ei, wci_ref[r, h])
            acc = acc + _dot(lhs, wk_ref[r, h, 0:half * (r + 1), :])
            y = jax.nn.gelu(acc + d_ref[:, lanes] * u[:, lanes])
            for e in range(half // lane_blk):
                y_ref[0, (half // lane_blk) * h + e, pl.ds(r, m, stride=q), :] = (
                    y[:, lane_blk * e:lane_blk * (e + 1)])


def _ssm(u4, init_state, prep, m, last_row):
    b, rows, width = u4.shape
    q = SSM_Q
    kern = functools.partial(_ssm_kernel, m=m, last_row=last_row)
    return pl.pallas_call(
        kern,
        grid=(b, rows // m),
        in_specs=[
            pl.BlockSpec((1, m, width), lambda bi, i: (bi, i, 0)),
            _const_spec((SUBLANES, 2 * N_STATE)),
            _const_spec((8, SUBLANES, N_STATE)),
            _const_spec((16, q * 128, 256)),
            _const_spec((q, 2, 1024, 256)),
            _const_spec((q, 2, 1024, 256)),
            _const_spec((q, 2, q * 256, 256)),
            _const_spec((1, width)),
        ],
        out_specs=[
            pl.BlockSpec((1, SSM_WIDTH // 128, q * m, 128), lambda bi, i: (bi, 0, i, 0)),
            pl.BlockSpec((1, SUBLANES, 2 * N_STATE), lambda bi, i: (bi, 0, 0)),
        ],
        out_shape=[
            jax.ShapeDtypeStruct((b, SSM_WIDTH // 128, q * rows, 128), F32),
            jax.ShapeDtypeStruct((b, SUBLANES, 2 * N_STATE), F32),
        ],
        scratch_shapes=[
            pltpu.VMEM((m, 2 * N_STATE), F32),
            pltpu.VMEM((SUBLANES, 2 * N_STATE), F32),
        ],
        compiler_params=_params(2),
    )(u4, init_state, prep["tab"], prep["wb"], prep["wcr"], prep["wci"], prep["wk"], prep["d"])


def _ssm_prep(lam_re, lam_im, log_step, b_re, b_im, c_re, c_im, d_skip):
    q = SSM_Q
    hp = lax.Precision.HIGHEST
    lr = lam_re.astype(F32)
    li = lam_im.astype(F32)
    step = jnp.exp(log_step.astype(F32))[:, None]
    mag = jnp.exp(lr * step)
    ar = mag * jnp.cos(li * step)
    ai = mag * jnp.sin(li * step)
    den = lr * lr + li * li
    cr = ((ar - 1.0) * lr + ai * li) / den
    ci = (ai * lr - (ar - 1.0) * li) / den
    br_ = b_re.astype(F32)
    bi_ = b_im.astype(F32)
    bbr = cr[..., None] * br_ - ci[..., None] * bi_
    bbi = cr[..., None] * bi_ + ci[..., None] * br_

    def powers(xr, xi, n):
        outr, outi = [jnp.ones_like(xr)], [jnp.zeros_like(xi)]
        for _ in range(n):
            pr, pi = outr[-1], outi[-1]
            outr.append(pr * xr - pi * xi)
            outi.append(pr * xi + pi * xr)
        return outr, outi

    pr, pi = powers(ar, ai, q)

    vre = jnp.stack([pr[q - 1 - s][..., None] * bbr - pi[q - 1 - s][..., None] * bbi for s in range(q)])
    vim = jnp.stack([pr[q - 1 - s][..., None] * bbi + pi[q - 1 - s][..., None] * bbr for s in range(q)])
    val = jnp.stack([vre, vim])
    val = val.reshape(2, q, 8, 4, SSM_STATE, SSM_GROUP)
    nb = jnp.arange(8)[:, None, None]
    gl = jnp.arange(8)[None, :, None]
    gq = jnp.arange(4)[None, None, :]
    mask = (gl == 4 * (nb % 2) + gq).astype(F32)
    wb = jnp.einsum("asnqpc,nlq->anslcqp", val, mask)
    wb = wb.reshape(16, q * 128, 256).astype(BF16)

    cre = c_re.astype(F32)
    cim = c_im.astype(F32)
    care = jnp.stack([cre * pr[t][:, None, :] - cim * pi[t][:, None, :] for t in range(q + 1)])
    caim = jnp.stack([cre * pi[t][:, None, :] + cim * pr[t][:, None, :] for t in range(q + 1)])
    eye = jnp.eye(16, dtype=F32)

    def pack_c(cv):
        cv = cv.reshape(q, 2, 16, SSM_GROUP, SSM_STATE)
        return jnp.einsum("rhgcp,gl->rhgplc", cv, eye).reshape(q, 2, 1024, 256).astype(BF16)

    wcr = pack_c(care[1:])
    wci = pack_c(-caim[1:])

    kt = (jnp.einsum("tgcp,gpd->tgcd", care[:q], bbr, precision=hp)
          - jnp.einsum("tgcp,gpd->tgcd", caim[:q], bbi, precision=hp))
    kt = kt.reshape(q, 2, 16, SSM_GROUP, SSM_GROUP)
    tt = jnp.einsum("thgcd,gl->thgdlc", kt, eye).reshape(q, 2, 256, 256)
    zero = jnp.zeros((2, 256, 256), F32)
    wk = jnp.stack([
        jnp.concatenate([tt[r - s] if s <= r else zero for s in range(q)], axis=1)
        for r in range(q)]).astype(BF16)

    aqr, aqi = powers(pr[q].reshape(1, N_STATE), pi[q].reshape(1, N_STATE), SUBLANES)
    rows = jnp.arange(SUBLANES)[:, None]
    tabs = []
    for d in (1, 2, 4):
        keep = (rows >= d).astype(F32)
        tabs += [keep * aqr[d], keep * aqi[d]]
    tabs += [jnp.concatenate(aqr[1:], axis=0), jnp.concatenate(aqi[1:], axis=0)]
    tab = jnp.stack(tabs)

    d4 = jnp.tile(d_skip.astype(F32).reshape(1, SSM_WIDTH), (1, q))
    return dict(wb=wb, wcr=wcr, wci=wci, wk=wk, tab=tab, d=d4)


def _mix_kernel(n_ref, y_ref, cg_ref, hist_ref, wg_ref, wglu_ref, wsp_ref,
                dww_ref, dwb_ref, lng_ref, lnb_ref, wcp_ref,
                merged_ref, hist_out_ref, buf, conv_scr, cb_scr, gate_scr, *, tt):
    hist_rows = CONV_HIST * SUBLANES

    @pl.when(pl.program_id(1) == 0)
    def _():
        buf[0:hist_rows, :] = hist_ref[...]

    @pl.when(pl.program_id(1) != 0)
    def _():
        buf[0:hist_rows, :] = buf[tt * SUBLANES:tt * SUBLANES + hist_rows, :]

    buf[hist_rows:hist_rows + tt * SUBLANES, :] = cg_ref[0]
    hist_out_ref[0] = buf[tt * SUBLANES:tt * SUBLANES + hist_rows, :]

    tb = 16
    off0 = CONV_HIST - (CONV_K - 1)
    blk = 512
    n = n_ref[0]

    def conv_block(base):
        for sub in range(base, base + tb, SUBLANES):
            accs = [dwb_ref[...] for _ in range(SUBLANES)]
            for k in range(CONV_K):
                wk = dww_ref[k]
                for t in range(SUBLANES):
                    r0 = (sub + t + off0 + k) * SUBLANES
                    accs[t] = accs[t] + buf[r0:r0 + SUBLANES, :] * wk
            for t in range(SUBLANES):
                r0 = (sub + t) * SUBLANES
                conv_scr[r0:r0 + SUBLANES, :] = accs[t]

        c = jnp.concatenate(
            [conv_scr[pl.ds(base * SUBLANES + s, tb, stride=SUBLANES), :] for s in range(SUBLANES)],
            axis=1)
        mu = jnp.mean(c, axis=-1, keepdims=True)
        cen = c - mu
        var = jnp.mean(cen * cen, axis=-1, keepdims=True)
        ln = cen * lax.rsqrt(var + LN_EPS) * lng_ref[...] + lnb_ref[...]
        cb_scr[base:base + tb, :] = (ln * jax.nn.sigmoid(ln)).astype(BF16)

    n_gate = 2 * D_MODEL // blk
    n_conv = tt // tb
    for c in range(n_gate):
        cols = slice(blk * c, blk * (c + 1))
        gate_scr[:, cols] = jax.nn.sigmoid(_dot(n, wg_ref[:, cols]))
        for cbi in range(c * n_conv // n_gate, (c + 1) * n_conv // n_gate):
            conv_block(cbi * tb)

    y = jnp.concatenate([y_ref[0, cbk] for cbk in range(SSM_WIDTH // 128)], axis=1)
    yg = (y * jax.nn.sigmoid(_dot(y.astype(BF16), wglu_ref[...]))).astype(BF16)
    cb = cb_scr[...]
    for j in range(D_MODEL // blk):
        cols = slice(blk * j, blk * (j + 1))
        gcols = slice(D_MODEL + blk * j, D_MODEL + blk * (j + 1))
        ba = _dot(yg, wsp_ref[:, cols])
        bb = _dot(cb, wcp_ref[:, cols])
        merged_ref[0, :, cols] = (gate_scr[:, cols] * ba + gate_scr[:, gcols] * bb).astype(BF16)


def _mix(n, y, cg, hist, wg, wglu, wsp, dww, dwb, lng, lnb, wcp, tt):
    b, l, _ = n.shape
    kern = functools.partial(_mix_kernel, tt=tt)
    tok = lambda width: pl.BlockSpec((1, tt, width), lambda bi, i: (bi, i, 0))
    return pl.pallas_call(
        kern,
        grid=(b, l // tt),
        in_specs=[
            tok(D_MODEL),
            pl.BlockSpec((1, SSM_WIDTH // 128, tt, 128), lambda bi, i: (bi, 0, i, 0)),
            pl.BlockSpec((1, tt * SUBLANES, CONV_LANES), lambda bi, i: (bi, i, 0)),
            _const_spec((CONV_HIST * SUBLANES, CONV_LANES)),
            _const_spec((D_MODEL, 2 * D_MODEL)),
            _const_spec((SSM_WIDTH, SSM_WIDTH)),
            _const_spec((SSM_WIDTH, D_MODEL)),
            _const_spec((CONV_K, SUBLANES, CONV_LANES)),
            _const_spec((SUBLANES, CONV_LANES)),
            _const_spec((1, CONV_WIDTH)),
            _const_spec((1, CONV_WIDTH)),
            _const_spec((CONV_WIDTH, D_MODEL)),
        ],
        out_specs=[
            tok(D_MODEL),
            pl.BlockSpec((1, CONV_HIST * SUBLANES, CONV_LANES), lambda bi, i: (bi, 0, 0)),
        ],
        out_shape=[
            jax.ShapeDtypeStruct((b, l, D_MODEL), BF16),
            jax.ShapeDtypeStruct((b, CONV_HIST * SUBLANES, CONV_LANES), F32),
        ],
        scratch_shapes=[
            pltpu.VMEM(((CONV_HIST + tt) * SUBLANES, CONV_LANES), F32),
            pltpu.VMEM((tt * SUBLANES, CONV_LANES), F32),
            pltpu.VMEM((tt, CONV_WIDTH), BF16),
            pltpu.VMEM((tt, 2 * D_MODEL), F32),
        ],
        compiler_params=_params(2, MIX_VMEM_LIMIT),
    )(n, y, cg, hist, wg, wglu, wsp,
      dww.reshape(CONV_K, SUBLANES, CONV_LANES), dwb.reshape(SUBLANES, CONV_LANES), lng, lnb, wcp)


def _mix_out_kernel(x_ref, m_ref, w_ref, g_ref, g2_ref, h_ref, n2_ref):
    o = _dot(m_ref[0], w_ref[...])
    h = x_ref[0] + _rms(o, g_ref[...])
    h_ref[0] = h
    n2_ref[0] = _rms(h, g2_ref[...]).astype(BF16)


def _mix_out(x, merged, w, g, g2, tt):
    b, l, _ = x.shape
    tok = pl.BlockSpec((1, tt, D_MODEL), lambda bi, i: (bi, i, 0))
    return pl.pallas_call(
        _mix_out_kernel,
        grid=(b, l // tt),
        in_specs=[tok, tok, _const_spec((D_MODEL, D_MODEL)), _const_spec((1, D_MODEL)),
                  _const_spec((1, D_MODEL))],
        out_specs=[tok, tok],
        out_shape=[jax.ShapeDtypeStruct((b, l, D_MODEL), F32),
                   jax.ShapeDtypeStruct((b, l, D_MODEL), BF16)],
        compiler_params=_params(2),
    )(x, merged, w, g, g2)


def _ffn_kernel(h_ref, n_ref, hist_ref, wg_ref, wv_ref, dwg_ref, dwv_ref, bg_ref, bv_ref,
                wd_ref, gpost_ref, out_ref, hist_out_ref, acc_scr, ubuf, act_scr, carry, *, tt, nf):
    i = pl.program_id(1)
    j = pl.program_id(2)
    fb = FFN_BLOCK

    @pl.when(j == 0)
    def _():
        acc_scr[...] = jnp.zeros_like(acc_scr)

    @pl.when(i == 0)
    def _():
        carry[j] = hist_ref[j]

    ubuf[0:FFN_HIST, :] = carry[j]
    taps = [jnp.concatenate([dwg_ref[k:k + 1, :], dwv_ref[k:k + 1, :]], axis=1) for k in range(FFN_K)]
    bias = jnp.concatenate([bg_ref[...], bv_ref[...]], axis=1)

    slab = min(tt, FFN_SLAB)

    def up(r0):
        n = n_ref[0, r0:r0 + slab, :]
        ubuf[FFN_HIST + r0:FFN_HIST + r0 + slab, 0:fb] = _dot(n, wg_ref[...])
        ubuf[FFN_HIST + r0:FFN_HIST + r0 + slab, fb:2 * fb] = _dot(n, wv_ref[...])

    up(0)
    for r0 in range(0, tt, slab):
        if r0 + slab < tt:
            up(r0 + slab)
        for c0 in range(r0, r0 + slab, FFN_CHUNK):
            win = ubuf[c0:c0 + FFN_HIST + FFN_CHUNK, :]
            conv = bias + win[FFN_HIST:] * taps[FFN_K - 1]
            for k in range(FFN_K - 1):
                d = FFN_K - 1 - k
                conv = conv + pltpu.roll(win, d, 0)[FFN_HIST:] * taps[k]
            act_scr[c0:c0 + FFN_CHUNK, :] = (
                jax.nn.gelu(conv[:, 0:fb]) * conv[:, fb:2 * fb]).astype(BF16)
        acc_scr[r0:r0 + slab, :] += _dot(act_scr[r0:r0 + slab, :], wd_ref[...])

    tail = ubuf[tt:tt + FFN_HIST, :]
    carry[j] = tail
    hist_out_ref[0, j] = tail

    @pl.when(j == nf - 1)
    def _():
        out_ref[0] = h_ref[0] + _rms(acc_scr[...], gpost_ref[...])


def _ffn(h, n2, hist, wup, dww, dwb, wdown, gpost, tt):
    b, l, _ = h.shape
    fb = FFN_BLOCK
    nf = D_FF // fb
    kern = functools.partial(_ffn_kernel, tt=tt, nf=nf)
    tok = pl.BlockSpec((1, tt, D_MODEL), lambda bi, i, j: (bi, i, 0))
    return pl.pallas_call(
        kern,
        grid=(b, l // tt, nf),
        in_specs=[
            tok,
            tok,
            _const_spec((nf, FFN_HIST, 2 * fb)),
            pl.BlockSpec((D_MODEL, fb), lambda bi, i, j: (0, j)),
            pl.BlockSpec((D_MODEL, fb), lambda bi, i, j: (0, nf + j)),
            pl.BlockSpec((FFN_K, fb), lambda bi, i, j: (0, j)),
            pl.BlockSpec((FFN_K, fb), lambda bi, i, j: (0, nf + j)),
            pl.BlockSpec((1, fb), lambda bi, i, j: (0, j)),
            pl.BlockSpec((1, fb), lambda bi, i, j: (0, nf + j)),
            pl.BlockSpec((fb, D_MODEL), lambda bi, i, j: (j, 0)),
            _const_spec((1, D_MODEL)),
        ],
        out_specs=[
            tok,
            pl.BlockSpec((1, nf, FFN_HIST, 2 * fb), lambda bi, i, j: (bi, 0, 0, 0)),
        ],
        out_shape=[
            jax.ShapeDtypeStruct((b, l, D_MODEL), F32),
            jax.ShapeDtypeStruct((b, nf, FFN_HIST, 2 * fb), F32),
        ],
        scratch_shapes=[
            pltpu.VMEM((tt, D_MODEL), F32),
            pltpu.VMEM((FFN_HIST + tt, 2 * fb), F32),
            pltpu.VMEM((tt, fb), BF16),
            pltpu.VMEM((nf, FFN_HIST, 2 * fb), F32),
        ],
        compiler_params=_params(3),
    )(h, n2, hist, wup, wup, dww, dww, dwb, dwb, wdown, gpost)


def _block(x, carries, p, ssm, tiles):
    b, l, _ = x.shape
    q = SSM_Q
    tt_in, m_ssm, tt_mix, tt_out, tt_ffn = tiles
    state0, conv_hist0, ffn_hist0 = carries

    n, u4, cg = _in_proj(x, p["g_mix_pre"], p["w_in_a"], tt_in)

    rows = l // q
    pad = (-rows) % m_ssm
    if pad:
        u4 = jnp.pad(u4, ((0, 0), (0, pad), (0, 0)))
    last_row = (rows - 1) % SUBLANES
    y, state = _ssm(u4, state0, ssm, m_ssm, last_row)
    if pad:
        y = y[:, :, :l]

    merged, conv_hist = _mix(n, y, cg, conv_hist0, p["w_in_g"], p["w_glu"],
                             p["w_sp"], p["conv_w"], p["conv_b"], p["ln_g"], p["ln_b"],
                             p["w_cp"], tt_mix)
    h1, n2 = _mix_out(x, merged, p["w_mo"], p["g_mix_post"], p["g_ffn_pre"], tt_out)
    out, ffn_hist = _ffn(h1, n2, ffn_hist0, p["w_up"], p["ffn_w"], p["ffn_b"],
                         p["w_down"], p["g_ffn_post"], tt_ffn)
    return out, (state[0], conv_hist[0], ffn_hist[0])


def kernel(x, meta_tokens, norm_mix_pre, w_in, lam_re, lam_im, log_step, ssm_b_re, ssm_b_im,
           ssm_c_re, ssm_c_im, ssm_d, w_ssm_glu, w_ssm_proj, conv_dw_w, conv_dw_b, conv_ln_g,
           conv_ln_b, w_conv_proj, w_mix_out, norm_mix_post, norm_ffn_pre, w_ffn_up, ffn_dw_w,
           ffn_dw_b, w_ffn_down, norm_ffn_post):
    depth = w_in.shape[0]
    n_a = SSM_WIDTH + 2 * CONV_WIDTH
    nf = D_FF // FFN_BLOCK
    zero_carries = (
        jnp.zeros((SUBLANES, 2 * N_STATE), F32),
        jnp.zeros((CONV_HIST * SUBLANES, CONV_LANES), F32),
        jnp.zeros((nf, FFN_HIST, 2 * FFN_BLOCK), F32),
    )
    h_meta = meta_tokens.astype(x.dtype)[None]
    h = x
    row = lambda v: v.astype(F32).reshape(1, -1)
    for i in range(depth):
        p = dict(
            g_mix_pre=row(norm_mix_pre[i]),
            w_in_a=w_in[i][:, :n_a].astype(BF16),
            w_in_g=w_in[i][:, n_a:].astype(BF16),
            w_glu=w_ssm_glu[i].astype(BF16),
            w_sp=w_ssm_proj[i].astype(BF16),
            conv_w=conv_dw_w[i].astype(F32),
            conv_b=row(conv_dw_b[i]),
            ln_g=row(conv_ln_g[i]),
            ln_b=row(conv_ln_b[i]),
            w_cp=w_conv_proj[i].astype(BF16),
            w_mo=w_mix_out[i].astype(BF16),
            g_mix_post=row(norm_mix_post[i]),
            g_ffn_pre=row(norm_ffn_pre[i]),
            w_up=w_ffn_up[i].astype(BF16),
            ffn_w=ffn_dw_w[i].astype(F32),
            ffn_b=row(ffn_dw_b[i]),
            w_down=w_ffn_down[i].astype(BF16),
            g_ffn_post=row(norm_ffn_post[i]),
        )
        ssm = _ssm_prep(lam_re[i], lam_im[i], log_step[i], ssm_b_re[i], ssm_b_im[i],
                        ssm_c_re[i], ssm_c_im[i], ssm_d[i])
        h_meta, carries = _block(h_meta, zero_carries, p, ssm,
                                 (N_META, SUBLANES, N_META, N_META, N_META))
        h, _ = _block(h, carries, p, ssm, (512, 256, 512, 512, 512))
    return h
```

```python
import functools

import jax
import jax.numpy as jnp
from jax import lax
from jax.experimental import pallas as pl
from jax.experimental.pallas import tpu as pltpu

F32 = jnp.float32
BF16 = jnp.bfloat16

D_MODEL = 2048
N_META = 16
SSM_WIDTH = 512
SSM_GROUP = 16
SSM_GROUPS = 32
SSM_STATE = 64
N_STATE = SSM_GROUPS * SSM_STATE
CONV_WIDTH = 1024
CONV_K = 31
D_FF = 5632
FFN_K = 3
NORM_EPS = 1e-6
LN_EPS = 1e-5

SSM_Q = 4
CONV_HIST = 32
FFN_HIST = 8
FFN_BLOCK = 512
FFN_SLAB = 512
FFN_CHUNK = 16
SUBLANES = 8
CONV_LANES = CONV_WIDTH // SUBLANES

VMEM_LIMIT = 56 * 1024 * 1024
MIX_VMEM_LIMIT = 60 * 1024 * 1024
FFN_VMEM_LIMIT = 62 * 1024 * 1024


def _params(n_axes, vmem_limit=VMEM_LIMIT):
    return pltpu.CompilerParams(
        dimension_semantics=("arbitrary",) * n_axes,
        vmem_limit_bytes=vmem_limit)


def _const_spec(shape):
    zeros = (0,) * len(shape)
    return pl.BlockSpec(shape, lambda *_: zeros, pipeline_mode=pl.Buffered(1))


def _rms(x, g):
    ms = jnp.mean(x * x, axis=-1, keepdims=True)
    return x * lax.rsqrt(ms + NORM_EPS) * g


def _dot(a, b):
    return jnp.dot(a, b, preferred_element_type=F32)


def _in_proj_kernel(x_ref, g_ref, w_ref, n_ref, u_ref, cg_ref, u_scr, *, tt):
    q = SSM_Q
    lanes = 128
    n = _rms(x_ref[0], g_ref[...]).astype(BF16)
    n_ref[0] = n
    u = _dot(n, w_ref[:, 0:SSM_WIDTH])
    for cb in range(SSM_WIDTH // lanes):
        u_scr[cb] = u[:, lanes * cb:lanes * (cb + 1)]
    for s in range(q):
        for cb in range(SSM_WIDTH // lanes):
            c0 = SSM_WIDTH * s + lanes * cb
            u_ref[0, :, c0:c0 + lanes] = u_scr[cb, pl.ds(s, tt // q, stride=q), :]
    blk = 512
    for j in range(CONV_WIDTH // blk):
        v0 = SSM_WIDTH + blk * j
        g0 = SSM_WIDTH + CONV_WIDTH + blk * j
        val = _dot(n, w_ref[:, v0:v0 + blk])
        gate = _dot(n, w_ref[:, g0:g0 + blk])
        cg = val * jax.nn.sigmoid(gate)
        for e in range(blk // lanes):
            s = j * (blk // lanes) + e
            cg_ref[0, pl.ds(s, tt, stride=SUBLANES), :] = cg[:, lanes * e:lanes * (e + 1)]


def _in_proj(x, g, w, tt):
    b, l, _ = x.shape
    ncol = w.shape[1]
    q = SSM_Q
    return pl.pallas_call(
        functools.partial(_in_proj_kernel, tt=tt),
        grid=(b, l // tt),
        in_specs=[
            pl.BlockSpec((1, tt, D_MODEL), lambda bi, i: (bi, i, 0)),
            _const_spec((1, D_MODEL)),
            _const_spec((D_MODEL, ncol)),
        ],
        out_specs=[
            pl.BlockSpec((1, tt, D_MODEL), lambda bi, i: (bi, i, 0)),
            pl.BlockSpec((1, tt // q, q * SSM_WIDTH), lambda bi, i: (bi, i, 0)),
            pl.BlockSpec((1, tt * SUBLANES, CONV_LANES), lambda bi, i: (bi, i, 0)),
        ],
        out_shape=[
            jax.ShapeDtypeStruct((b, l, D_MODEL), BF16),
            jax.ShapeDtypeStruct((b, l // q, q * SSM_WIDTH), F32),
            jax.ShapeDtypeStruct((b, l * SUBLANES, CONV_LANES), F32),
        ],
        scratch_shapes=[pltpu.VMEM((SSM_WIDTH // 128, tt, 128), F32)],
        compiler_params=_params(2),
    )(x, g, w)


def _ssm_kernel(u_ref, init_ref, tab_ref, wb_ref, wcr_ref, wci_ref, wk_ref, d_ref,
                y_ref, state_out_ref, x_scr, state_scr, *, m, last_row):
    q = SSM_Q
    lane_blk = 128
    tile_n = 256

    @pl.when(pl.program_id(1) == 0)
    def _():
        state_scr[...] = init_ref[...]

    u = u_ref[0]
    ub = u.astype(BF16)

    for n in range(2 * N_STATE // tile_n):
        kb = (n % (N_STATE // tile_n)) // 2
        lhs = jnp.concatenate(
            [ub[:, SSM_WIDTH * s + lane_blk * kb:SSM_WIDTH * s + lane_blk * (kb + 1)]
             for s in range(q)], axis=1)
        x_scr[:, tile_n * n:tile_n * (n + 1)] = _dot(lhs, wb_ref[n])

    w = 512
    row_id = lax.broadcasted_iota(jnp.int32, (SUBLANES, w), 0)
    for j in range(N_STATE // w):
        sr = slice(w * j, w * (j + 1))
        si = slice(N_STATE + w * j, N_STATE + w * (j + 1))
        tabs = [tab_ref[t, :, sr] for t in range(8)]

        def body(rb, carry, sr=sr, si=si, tabs=tabs):
            cr, ci = carry
            rows = pl.ds(pl.multiple_of(rb * SUBLANES, SUBLANES), SUBLANES)
            xr = x_scr[rows, sr]
            xi = x_scr[rows, si]
            for lvl, d in enumerate((1, 2, 4)):
                ar, ai = tabs[2 * lvl], tabs[2 * lvl + 1]
                pr = pltpu.roll(xr, d, 0)
                pi = pltpu.roll(xi, d, 0)
                xr, xi = xr + ar * pr - ai * pi, xi + ar * pi + ai * pr
            ar, ai = tabs[6], tabs[7]
            xr, xi = xr + ar * cr - ai * ci, xi + ar * ci + ai * cr
            x_scr[rows, sr] = jnp.where(row_id == 0, cr, pltpu.roll(xr, 1, 0))
            x_scr[rows, si] = jnp.where(row_id == 0, ci, pltpu.roll(xi, 1, 0))
            state_scr[:, sr] = jnp.broadcast_to(xr[last_row:last_row + 1], (SUBLANES, w))
            state_scr[:, si] = jnp.broadcast_to(xi[last_row:last_row + 1], (SUBLANES, w))
            return (jnp.broadcast_to(xr[SUBLANES - 1:SUBLANES], (SUBLANES, w)),
                    jnp.broadcast_to(xi[SUBLANES - 1:SUBLANES], (SUBLANES, w)))

        lax.fori_loop(0, m // SUBLANES, body, (state_scr[:, sr], state_scr[:, si]))

    state_out_ref[0] = state_scr[...]

    half = 256
    st_per_half = half // SSM_GROUP * SSM_STATE
    for h in range(SSM_WIDTH // half):
        er = x_scr[:, st_per_half * h:st_per_half * (h + 1)].astype(BF16)
        ei = x_scr[:, N_STATE + st_per_half * h:N_STATE + st_per_half * (h + 1)].astype(BF16)
        for r in range(q):
            lanes = slice(SSM_WIDTH * r + half * h, SSM_WIDTH * r + half * (h + 1))
            lhs = jnp.concatenate(
                [ub[:, SSM_WIDTH * s + half * h:SSM_WIDTH * s + half * (h + 1)]
                 for s in range(r + 1)], axis=1)
            acc = _dot(er, wcr_ref[r, h]) + _dot(ei, wci_ref[r, h])
            acc = acc + _dot(lhs, wk_ref[r, h, 0:half * (r + 1), :])
            y = jax.nn.gelu(acc + d_ref[:, lanes] * u[:, lanes])
            for e in range(half // lane_blk):
                y_ref[0, (half // lane_blk) * h + e, pl.ds(r, m, stride=q), :] = (
                    y[:, lane_blk * e:lane_blk * (e + 1)])


def _ssm(u4, init_state, prep, m, last_row):
    b, rows, width = u4.shape
    q = SSM_Q
    kern = functools.partial(_ssm_kernel, m=m, last_row=last_row)
    return pl.pallas_call(
        kern,
        grid=(b, rows // m),
        in_specs=[
            pl.BlockSpec((1, m, width), lambda bi, i: (bi, i, 0)),
            _const_spec((SUBLANES, 2 * N_STATE)),
            _const_spec((8, SUBLANES, N_STATE)),
            _const_spec((16, q * 128, 256)),
            _const_spec((q, 2, 1024, 256)),
            _const_spec((q, 2, 1024, 256)),
            _const_spec((q, 2, q * 256, 256)),
            _const_spec((1, width)),
        ],
        out_specs=[
            pl.BlockSpec((1, SSM_WIDTH // 128, q * m, 128), lambda bi, i: (bi, 0, i, 0)),
            pl.BlockSpec((1, SUBLANES, 2 * N_STATE), lambda bi, i: (bi, 0, 0)),
        ],
        out_shape=[
            jax.ShapeDtypeStruct((b, SSM_WIDTH // 128, q * rows, 128), F32),
            jax.ShapeDtypeStruct((b, SUBLANES, 2 * N_STATE), F32),
        ],
        scratch_shapes=[
            pltpu.VMEM((m, 2 * N_STATE), F32),
            pltpu.VMEM((SUBLANES, 2 * N_STATE), F32),
        ],
        compiler_params=_params(2),
    )(u4, init_state, prep["tab"], prep["wb"], prep["wcr"], prep["wci"], prep["wk"], prep["d"])


def _ssm_prep(lam_re, lam_im, log_step, b_re, b_im, c_re, c_im, d_skip):
    q = SSM_Q
    hp = lax.Precision.HIGHEST
    lr = lam_re.astype(F32)
    li = lam_im.astype(F32)
    step = jnp.exp(log_step.astype(F32))[:, None]
    mag = jnp.exp(lr * step)
    ar = mag * jnp.cos(li * step)
    ai = mag * jnp.sin(li * step)
    den = lr * lr + li * li
    cr = ((ar - 1.0) * lr + ai * li) / den
    ci = (ai * lr - (ar - 1.0) * li) / den
    br_ = b_re.astype(F32)
    bi_ = b_im.astype(F32)
    bbr = cr[..., None] * br_ - ci[..., None] * bi_
    bbi = cr[..., None] * bi_ + ci[..., None] * br_

    def powers(xr, xi, n):
        outr, outi = [jnp.ones_like(xr)], [jnp.zeros_like(xi)]
        for _ in range(n):
            pr, pi = outr[-1], outi[-1]
            outr.append(pr * xr - pi * xi)
            outi.append(pr * xi + pi * xr)
        return outr, outi

    pr, pi = powers(ar, ai, q)

    vre = jnp.stack([pr[q - 1 - s][..., None] * bbr - pi[q - 1 - s][..., None] * bbi for s in range(q)])
    vim = jnp.stack([pr[q - 1 - s][..., None] * bbi + pi[q - 1 - s][..., None] * bbr for s in range(q)])
    val = jnp.stack([vre, vim])
    val = val.reshape(2, q, 8, 4, SSM_STATE, SSM_GROUP)
    nb = jnp.arange(8)[:, None, None]
    gl = jnp.arange(8)[None, :, None]
    gq = jnp.arange(4)[None, None, :]
    mask = (gl == 4 * (nb % 2) + gq).astype(F32)
    wb = jnp.einsum("asnqpc,nlq->anslcqp", val, mask)
    wb = wb.reshape(16, q * 128, 256).astype(BF16)

    cre = c_re.astype(F32)
    cim = c_im.astype(F32)
    care = jnp.stack([cre * pr[t][:, None, :] - cim * pi[t][:, None, :] for t in range(q + 1)])
    caim = jnp.stack([cre * pi[t][:, None, :] + cim * pr[t][:, None, :] for t in range(q + 1)])
    eye = jnp.eye(16, dtype=F32)

    def pack_c(cv):
        cv = cv.reshape(q, 2, 16, SSM_GROUP, SSM_STATE)
        return jnp.einsum("rhgcp,gl->rhgplc", cv, eye).reshape(q, 2, 1024, 256).astype(BF16)

    wcr = pack_c(care[1:])
    wci = pack_c(-caim[1:])

    kt = (jnp.einsum("tgcp,gpd->tgcd", care[:q], bbr, precision=hp)
          - jnp.einsum("tgcp,gpd->tgcd", caim[:q], bbi, precision=hp))
    kt = kt.reshape(q, 2, 16, SSM_GROUP, SSM_GROUP)
    tt = jnp.einsum("thgcd,gl->thgdlc", kt, eye).reshape(q, 2, 256, 256)
    zero = jnp.zeros((2, 256, 256), F32)
    wk = jnp.stack([
        jnp.concatenate([tt[r - s] if s <= r else zero for s in range(q)], axis=1)
        for r in range(q)]).astype(BF16)

    aqr, aqi = powers(pr[q].reshape(1, N_STATE), pi[q].reshape(1, N_STATE), SUBLANES)
    rows = jnp.arange(SUBLANES)[:, None]
    tabs = []
    for d in (1, 2, 4):
        keep = (rows >= d).astype(F32)
        tabs += [keep * aqr[d], keep * aqi[d]]
    tabs += [jnp.concatenate(aqr[1:], axis=0), jnp.concatenate(aqi[1:], axis=0)]
    tab = jnp.stack(tabs)

    d4 = jnp.tile(d_skip.astype(F32).reshape(1, SSM_WIDTH), (1, q))
    return dict(wb=wb, wcr=wcr, wci=wci, wk=wk, tab=tab, d=d4)


def _mix_kernel(n_ref, y_ref, cg_ref, hist_ref, wg_ref, wglu_ref, wsp_ref,
                dww_ref, dwb_ref, lng_ref, lnb_ref, wcp_ref,
                merged_ref, hist_out_ref, buf, conv_scr, cb_scr, gate_scr, *, tt):
    hist_rows = CONV_HIST * SUBLANES

    @pl.when(pl.program_id(1) == 0)
    def _():
        buf[0:hist_rows, :] = hist_ref[...]

    @pl.when(pl.program_id(1) != 0)
    def _():
        buf[0:hist_rows, :] = buf[tt * SUBLANES:tt * SUBLANES + hist_rows, :]

    buf[hist_rows:hist_rows + tt * SUBLANES, :] = cg_ref[0]
    hist_out_ref[0] = buf[tt * SUBLANES:tt * SUBLANES + hist_rows, :]

    tb = 16
    off0 = CONV_HIST - (CONV_K - 1)
    blk = 512
    n = n_ref[0]

    def conv_block(base):
        for sub in range(base, base + tb, SUBLANES):
            accs = [dwb_ref[...] for _ in range(SUBLANES)]
            for k in range(CONV_K):
                wk = dww_ref[k]
                for t in range(SUBLANES):
                    r0 = (sub + t + off0 + k) * SUBLANES
                    accs[t] = accs[t] + buf[r0:r0 + SUBLANES, :] * wk
            for t in range(SUBLANES):
                r0 = (sub + t) * SUBLANES
                conv_scr[r0:r0 + SUBLANES, :] = accs[t]

        c = jnp.concatenate(
            [conv_scr[pl.ds(base * SUBLANES + s, tb, stride=SUBLANES), :] for s in range(SUBLANES)],
            axis=1)
        mu = jnp.mean(c, axis=-1, keepdims=True)
        cen = c - mu
        var = jnp.mean(cen * cen, axis=-1, keepdims=True)
        ln = cen * lax.rsqrt(var + LN_EPS) * lng_ref[...] + lnb_ref[...]
        cb_scr[base:base + tb, :] = (ln * jax.nn.sigmoid(ln)).astype(BF16)

    n_gate = 2 * D_MODEL // blk
    n_conv = tt // tb
    for c in range(n_gate):
        cols = slice(blk * c, blk * (c + 1))
        gate_scr[:, cols] = jax.nn.sigmoid(_dot(n, wg_ref[:, cols]))
        for cbi in range(c * n_conv // n_gate, (c + 1) * n_conv // n_gate):
            conv_block(cbi * tb)

    y = jnp.concatenate([y_ref[0, cbk] for cbk in range(SSM_WIDTH // 128)], axis=1)
    yg = (y * jax.nn.sigmoid(_dot(y.astype(BF16), wglu_ref[...]))).astype(BF16)
    cb = cb_scr[...]
    for j in range(D_MODEL // blk):
        cols = slice(blk * j, blk * (j + 1))
        gcols = slice(D_MODEL + blk * j, D_MODEL + blk * (j + 1))
        ba = _dot(yg, wsp_ref[:, cols])
        bb = _dot(cb, wcp_ref[:, cols])
        merged_ref[0, :, cols] = (gate_scr[:, cols] * ba + gate_scr[:, gcols] * bb).astype(BF16)


def _mix(n, y, cg, hist, wg, wglu, wsp, dww, dwb, lng, lnb, wcp, tt):
    b, l, _ = n.shape
    kern = functools.partial(_mix_kernel, tt=tt)
    tok = lambda width: pl.BlockSpec((1, tt, width), lambda bi, i: (bi, i, 0))
    return pl.pallas_call(
        kern,
        grid=(b, l // tt),
        in_specs=[
            tok(D_MODEL),
            pl.BlockSpec((1, SSM_WIDTH // 128, tt, 128), lambda bi, i: (bi, 0, i, 0)),
            pl.BlockSpec((1, tt * SUBLANES, CONV_LANES), lambda bi, i: (bi, i, 0)),
            _const_spec((CONV_HIST * SUBLANES, CONV_LANES)),
            _const_spec((D_MODEL, 2 * D_MODEL)),
            _const_spec((SSM_WIDTH, SSM_WIDTH)),
            _const_spec((SSM_WIDTH, D_MODEL)),
            _const_spec((CONV_K, SUBLANES, CONV_LANES)),
            _const_spec((SUBLANES, CONV_LANES)),
            _const_spec((1, CONV_WIDTH)),
            _const_spec((1, CONV_WIDTH)),
            _const_spec((CONV_WIDTH, D_MODEL)),
        ],
        out_specs=[
            tok(D_MODEL),
            pl.BlockSpec((1, CONV_HIST * SUBLANES, CONV_LANES), lambda bi, i: (bi, 0, 0)),
        ],
        out_shape=[
            jax.ShapeDtypeStruct((b, l, D_MODEL), BF16),
            jax.ShapeDtypeStruct((b, CONV_HIST * SUBLANES, CONV_LANES), F32),
        ],
        scratch_shapes=[
            pltpu.VMEM(((CONV_HIST + tt) * SUBLANES, CONV_LANES), F32),
            pltpu.VMEM((tt * SUBLANES, CONV_LANES), F32),
            pltpu.VMEM((tt, CONV_WIDTH), BF16),
            pltpu.VMEM((tt, 2 * D_MODEL), F32),
        ],
        compiler_params=_params(2, MIX_VMEM_LIMIT),
    )(n, y, cg, hist, wg, wglu, wsp,
      dww.reshape(CONV_K, SUBLANES, CONV_LANES), dwb.reshape(SUBLANES, CONV_LANES), lng, lnb, wcp)


def _mix_out_kernel(x_ref, m_ref, w_ref, g_ref, g2_ref, h_ref, n2_ref):
    o = _dot(m_ref[0], w_ref[...])
    h = x_ref[0] + _rms(o, g_ref[...])
    h_ref[0] = h
    n2_ref[0] = _rms(h, g2_ref[...]).astype(BF16)


def _mix_out(x, merged, w, g, g2, tt):
    b, l, _ = x.shape
    tok = pl.BlockSpec((1, tt, D_MODEL), lambda bi, i: (bi, i, 0))
    return pl.pallas_call(
        _mix_out_kernel,
        grid=(b, l // tt),
        in_specs=[tok, tok, _const_spec((D_MODEL, D_MODEL)), _const_spec((1, D_MODEL)),
                  _const_spec((1, D_MODEL))],
        out_specs=[tok, tok],
        out_shape=[jax.ShapeDtypeStruct((b, l, D_MODEL), F32),
                   jax.ShapeDtypeStruct((b, l, D_MODEL), BF16)],
        compiler_params=_params(2),
    )(x, merged, w, g, g2)


def _ffn_kernel(h_ref, n_ref, hist_ref, wg_ref, wv_ref, dwg_ref, dwv_ref, bg_ref, bv_ref,
                wd_ref, gpost_ref, out_ref, hist_out_ref, ubuf, act_scr, carry, *, tt, nf):
    i = pl.program_id(1)
    j = pl.program_id(2)
    fb = FFN_BLOCK

    @pl.when(j == 0)
    def _():
        out_ref[0] = jnp.zeros((tt, D_MODEL), F32)

    @pl.when(i == 0)
    def _():
        carry[j] = hist_ref[j]

    ubuf[0:FFN_HIST, :] = carry[j]
    taps = [jnp.concatenate([dwg_ref[k:k + 1, :], dwv_ref[k:k + 1, :]], axis=1) for k in range(FFN_K)]
    bias = jnp.concatenate([bg_ref[...], bv_ref[...]], axis=1)

    slab = min(tt, FFN_SLAB)

    def up(r0):
        n = n_ref[0, r0:r0 + slab, :]
        ubuf[FFN_HIST + r0:FFN_HIST + r0 + slab, 0:fb] = _dot(n, wg_ref[...])
        ubuf[FFN_HIST + r0:FFN_HIST + r0 + slab, fb:2 * fb] = _dot(n, wv_ref[...])

    up(0)
    for r0 in range(0, tt, slab):
        if r0 + slab < tt:
            up(r0 + slab)
        for c0 in range(r0, r0 + slab, FFN_CHUNK):
            win = ubuf[c0:c0 + FFN_HIST + FFN_CHUNK, :]
            conv = bias + win[FFN_HIST:] * taps[FFN_K - 1]
            for k in range(FFN_K - 1):
                d = FFN_K - 1 - k
                conv = conv + pltpu.roll(win, d, 0)[FFN_HIST:] * taps[k]
            act_scr[c0:c0 + FFN_CHUNK, :] = (
                jax.nn.gelu(conv[:, 0:fb]) * conv[:, fb:2 * fb]).astype(BF16)
        out_ref[0, r0:r0 + slab, :] += _dot(act_scr[r0:r0 + slab, :], wd_ref[...])

    tail = ubuf[tt:tt + FFN_HIST, :]
    carry[j] = tail
    hist_out_ref[0, j] = tail

    @pl.when(j == nf - 1)
    def _():
        out_ref[0] = h_ref[0] + _rms(out_ref[0], gpost_ref[...])


def _ffn(h, n2, hist, wup, dww, dwb, wdown, gpost, tt):
    b, l, _ = h.shape
    fb = FFN_BLOCK
    nf = D_FF // fb
    kern = functools.partial(_ffn_kernel, tt=tt, nf=nf)
    tok = pl.BlockSpec((1, tt, D_MODEL), lambda bi, i, j: (bi, i, 0))
    return pl.pallas_call(
        kern,
        grid=(b, l // tt, nf),
        in_specs=[
            pl.BlockSpec((1, tt, D_MODEL), lambda bi, i, j: (bi, i, 0), pipeline_mode=pl.Buffered(1)),
            tok,
            _const_spec((nf, FFN_HIST, 2 * fb)),
            pl.BlockSpec((D_MODEL, fb), lambda bi, i, j: (0, j)),
            pl.BlockSpec((D_MODEL, fb), lambda bi, i, j: (0, nf + j)),
            pl.BlockSpec((FFN_K, fb), lambda bi, i, j: (0, j)),
            pl.BlockSpec((FFN_K, fb), lambda bi, i, j: (0, nf + j)),
            pl.BlockSpec((1, fb), lambda bi, i, j: (0, j)),
            pl.BlockSpec((1, fb), lambda bi, i, j: (0, nf + j)),
            pl.BlockSpec((fb, D_MODEL), lambda bi, i, j: (j, 0)),
            _const_spec((1, D_MODEL)),
        ],
        out_specs=[
            tok,
            pl.BlockSpec((1, nf, FFN_HIST, 2 * fb), lambda bi, i, j: (bi, 0, 0, 0)),
        ],
        out_shape=[
            jax.ShapeDtypeStruct((b, l, D_MODEL), F32),
            jax.ShapeDtypeStruct((b, nf, FFN_HIST, 2 * fb), F32),
        ],
        scratch_shapes=[
            pltpu.VMEM((FFN_HIST + tt, 2 * fb), F32),
            pltpu.VMEM((tt, fb), BF16),
            pltpu.VMEM((nf, FFN_HIST, 2 * fb), F32),
        ],
        compiler_params=_params(3, FFN_VMEM_LIMIT),
    )(h, n2, hist, wup, wup, dww, dww, dwb, dwb, wdown, gpost)


def _block(x, carries, p, ssm, tiles):
    b, l, _ = x.shape
    q = SSM_Q
    tt_in, m_ssm, tt_mix, tt_out, tt_ffn = tiles
    state0, conv_hist0, ffn_hist0 = carries

    n, u4, cg = _in_proj(x, p["g_mix_pre"], p["w_in_a"], tt_in)

    rows = l // q
    pad = (-rows) % m_ssm
    if pad:
        u4 = jnp.pad(u4, ((0, 0), (0, pad), (0, 0)))
    last_row = (rows - 1) % SUBLANES
    y, state = _ssm(u4, state0, ssm, m_ssm, last_row)
    if pad:
        y = y[:, :, :l]

    merged, conv_hist = _mix(n, y, cg, conv_hist0, p["w_in_g"], p["w_glu"],
                             p["w_sp"], p["conv_w"], p["conv_b"], p["ln_g"], p["ln_b"],
                             p["w_cp"], tt_mix)
    h1, n2 = _mix_out(x, merged, p["w_mo"], p["g_mix_post"], p["g_ffn_pre"], tt_out)
    out, ffn_hist = _ffn(h1, n2, ffn_hist0, p["w_up"], p["ffn_w"], p["ffn_b"],
                         p["w_down"], p["g_ffn_post"], tt_ffn)
    return out, (state[0], conv_hist[0], ffn_hist[0])


def kernel(x, meta_tokens, norm_mix_pre, w_in, lam_re, lam_im, log_step, ssm_b_re, ssm_b_im,
           ssm_c_re, ssm_c_im, ssm_d, w_ssm_glu, w_ssm_proj, conv_dw_w, conv_dw_b, conv_ln_g,
           conv_ln_b, w_conv_proj, w_mix_out, norm_mix_post, norm_ffn_pre, w_ffn_up, ffn_dw_w,
           ffn_dw_b, w_ffn_down, norm_ffn_post):
    depth = w_in.shape[0]
    n_a = SSM_WIDTH + 2 * CONV_WIDTH
    nf = D_FF // FFN_BLOCK
    zero_carries = (
        jnp.zeros((SUBLANES, 2 * N_STATE), F32),
        jnp.zeros((CONV_HIST * SUBLANES, CONV_LANES), F32),
        jnp.zeros((nf, FFN_HIST, 2 * FFN_BLOCK), F32),
    )
    h_meta = meta_tokens.astype(x.dtype)[None]
    h = x
    row = lambda v: v.astype(F32).reshape(1, -1)
    for i in range(depth):
        p = dict(
            g_mix_pre=row(norm_mix_pre[i]),
            w_in_a=w_in[i][:, :n_a].astype(BF16),
            w_in_g=w_in[i][:, n_a:].astype(BF16),
            w_glu=w_ssm_glu[i].astype(BF16),
            w_sp=w_ssm_proj[i].astype(BF16),
            conv_w=conv_dw_w[i].astype(F32),
            conv_b=row(conv_dw_b[i]),
            ln_g=row(conv_ln_g[i]),
            ln_b=row(conv_ln_b[i]),
            w_cp=w_conv_proj[i].astype(BF16),
            w_mo=w_mix_out[i].astype(BF16),
            g_mix_post=row(norm_mix_post[i]),
            g_ffn_pre=row(norm_ffn_pre[i]),
            w_up=w_ffn_up[i].astype(BF16),
            ffn_w=ffn_dw_w[i].astype(F32),
            ffn_b=row(ffn_dw_b[i]),
            w_down=w_ffn_down[i].astype(BF16),
            g_ffn_post=row(norm_ffn_post[i]),
        )
        ssm = _ssm_prep(lam_re[i], lam_im[i], log_step[i], ssm_b_re[i], ssm_b_im[i],
                        ssm_c_re[i], ssm_c_im[i], ssm_d[i])
        h_meta, carries = _block(h_meta, zero_carries, p, ssm,
                                 (N_META, SUBLANES, N_META, N_META, N_META))
        h, _ = _block(h, carries, p, ssm, (512, 256, 512, 512, 1024))
    return h
```

```python
import functools

import jax
import jax.numpy as jnp
from jax import lax
from jax.experimental import pallas as pl
from jax.experimental.pallas import tpu as pltpu

F32 = jnp.float32
BF16 = jnp.bfloat16

D_MODEL = 2048
N_META = 16
SSM_WIDTH = 512
SSM_GROUP = 16
SSM_GROUPS = 32
SSM_STATE = 64
N_STATE = SSM_GROUPS * SSM_STATE
CONV_WIDTH = 1024
CONV_K = 31
D_FF = 5632
FFN_K = 3
NORM_EPS = 1e-6
LN_EPS = 1e-5

SSM_Q = 4
CONV_HIST = 32
FFN_HIST = 8
FFN_BLOCK = 512
FFN_SLAB = 512
FFN_CHUNK = 16
SUBLANES = 8
CONV_LANES = CONV_WIDTH // SUBLANES

VMEM_LIMIT = 56 * 1024 * 1024
MIX_VMEM_LIMIT = 60 * 1024 * 1024
FFN_VMEM_LIMIT = 62 * 1024 * 1024


def _params(n_axes, vmem_limit=VMEM_LIMIT):
    return pltpu.CompilerParams(
        dimension_semantics=("arbitrary",) * n_axes,
        vmem_limit_bytes=vmem_limit)


def _const_spec(shape):
    zeros = (0,) * len(shape)
    return pl.BlockSpec(shape, lambda *_: zeros, pipeline_mode=pl.Buffered(1))


def _rms(x, g):
    ms = jnp.mean(x * x, axis=-1, keepdims=True)
    return x * lax.rsqrt(ms + NORM_EPS) * g


def _dot(a, b):
    return jnp.dot(a, b, preferred_element_type=F32)


def _in_proj_kernel(x_ref, g_ref, w_ref, n_ref, u_ref, cg_ref, u_scr, *, tt):
    q = SSM_Q
    lanes = 128
    n = _rms(x_ref[0], g_ref[...]).astype(BF16)
    n_ref[0] = n
    u = _dot(n, w_ref[:, 0:SSM_WIDTH])
    for cb in range(SSM_WIDTH // lanes):
        u_scr[cb] = u[:, lanes * cb:lanes * (cb + 1)]
    for s in range(q):
        for cb in range(SSM_WIDTH // lanes):
            c0 = SSM_WIDTH * s + lanes * cb
            u_ref[0, :, c0:c0 + lanes] = u_scr[cb, pl.ds(s, tt // q, stride=q), :]
    blk = 512
    for j in range(CONV_WIDTH // blk):
        v0 = SSM_WIDTH + blk * j
        g0 = SSM_WIDTH + CONV_WIDTH + blk * j
        val = _dot(n, w_ref[:, v0:v0 + blk])
        gate = _dot(n, w_ref[:, g0:g0 + blk])
        cg = val * jax.nn.sigmoid(gate)
        for e in range(blk // lanes):
            s = j * (blk // lanes) + e
            cg_ref[0, pl.ds(s, tt, stride=SUBLANES), :] = cg[:, lanes * e:lanes * (e + 1)]


def _in_proj(x, g, w, tt):
    b, l, _ = x.shape
    ncol = w.shape[1]
    q = SSM_Q
    return pl.pallas_call(
        functools.partial(_in_proj_kernel, tt=tt),
        grid=(b, l // tt),
        in_specs=[
            pl.BlockSpec((1, tt, D_MODEL), lambda bi, i: (bi, i, 0)),
            _const_spec((1, D_MODEL)),
            _const_spec((D_MODEL, ncol)),
        ],
        out_specs=[
            pl.BlockSpec((1, tt, D_MODEL), lambda bi, i: (bi, i, 0)),
            pl.BlockSpec((1, tt // q, q * SSM_WIDTH), lambda bi, i: (bi, i, 0)),
            pl.BlockSpec((1, tt * SUBLANES, CONV_LANES), lambda bi, i: (bi, i, 0)),
        ],
        out_shape=[
            jax.ShapeDtypeStruct((b, l, D_MODEL), BF16),
            jax.ShapeDtypeStruct((b, l // q, q * SSM_WIDTH), F32),
            jax.ShapeDtypeStruct((b, l * SUBLANES, CONV_LANES), F32),
        ],
        scratch_shapes=[pltpu.VMEM((SSM_WIDTH // 128, tt, 128), F32)],
        compiler_params=_params(2),
    )(x, g, w)


def _ssm_kernel(u_ref, init_ref, tab_ref, wb_ref, wcr_ref, wci_ref, wk_ref, d_ref,
                y_ref, state_out_ref, x_scr, state_scr, *, m, last_row):
    q = SSM_Q
    lane_blk = 128
    tile_n = 256

    @pl.when(pl.program_id(1) == 0)
    def _():
        state_scr[...] = init_ref[...]

    u = u_ref[0]
    ub = u.astype(BF16)

    for n in range(2 * N_STATE // tile_n):
        kb = (n % (N_STATE // tile_n)) // 2
        lhs = jnp.concatenate(
            [ub[:, SSM_WIDTH * s + lane_blk * kb:SSM_WIDTH * s + lane_blk * (kb + 1)]
             for s in range(q)], axis=1)
        x_scr[:, tile_n * n:tile_n * (n + 1)] = _dot(lhs, wb_ref[n])

    w = 512
    row_id = lax.broadcasted_iota(jnp.int32, (SUBLANES, w), 0)
    for j in range(N_STATE // w):
        sr = slice(w * j, w * (j + 1))
        si = slice(N_STATE + w * j, N_STATE + w * (j + 1))
        tabs = [tab_ref[t, :, sr] for t in range(8)]

        cr, ci = state_scr[:, sr], state_scr[:, si]
        for rb in range(m // SUBLANES):
            rows = slice(rb * SUBLANES, (rb + 1) * SUBLANES)
            xr = x_scr[rows, sr]
            xi = x_scr[rows, si]
            for lvl, d in enumerate((1, 2, 4)):
                ar, ai = tabs[2 * lvl], tabs[2 * lvl + 1]
                pr = pltpu.roll(xr, d, 0)
                pi = pltpu.roll(xi, d, 0)
                xr, xi = xr + ar * pr - ai * pi, xi + ar * pi + ai * pr
            ar, ai = tabs[6], tabs[7]
            xr, xi = xr + ar * cr - ai * ci, xi + ar * ci + ai * cr
            x_scr[rows, sr] = jnp.where(row_id == 0, cr, pltpu.roll(xr, 1, 0))
            x_scr[rows, si] = jnp.where(row_id == 0, ci, pltpu.roll(xi, 1, 0))
            if rb == m // SUBLANES - 1:
                state_scr[:, sr] = jnp.broadcast_to(xr[last_row:last_row + 1], (SUBLANES, w))
                state_scr[:, si] = jnp.broadcast_to(xi[last_row:last_row + 1], (SUBLANES, w))
            cr = jnp.broadcast_to(xr[SUBLANES - 1:SUBLANES], (SUBLANES, w))
            ci = jnp.broadcast_to(xi[SUBLANES - 1:SUBLANES], (SUBLANES, w))

    state_out_ref[0] = state_scr[...]

    half = 256
    st_per_half = half // SSM_GROUP * SSM_STATE
    for h in range(SSM_WIDTH // half):
        er = x_scr[:, st_per_half * h:st_per_half * (h + 1)].astype(BF16)
        ei = x_scr[:, N_STATE + st_per_half * h:N_STATE + st_per_half * (h + 1)].astype(BF16)
        for r in range(q):
            lanes = slice(SSM_WIDTH * r + half * h, SSM_WIDTH * r + half * (h + 1))
            lhs = jnp.concatenate(
                [ub[:, SSM_WIDTH * s + half * h:SSM_WIDTH * s + half * (h + 1)]
                 for s in range(r + 1)], axis=1)
            acc = _dot(er, wcr_ref[r, h]) + _dot(ei, wci_ref[r, h])
            acc = acc + _dot(lhs, wk_ref[r, h, 0:half * (r + 1), :])
            y = jax.nn.gelu(acc + d_ref[:, lanes] * u[:, lanes])
            for e in range(half // lane_blk):
                y_ref[0, (half // lane_blk) * h + e, pl.ds(r, m, stride=q), :] = (
                    y[:, lane_blk * e:lane_blk * (e + 1)])


def _ssm(u4, init_state, prep, m, last_row):
    b, rows, width = u4.shape
    q = SSM_Q
    kern = functools.partial(_ssm_kernel, m=m, last_row=last_row)
    return pl.pallas_call(
        kern,
        grid=(b, rows // m),
        in_specs=[
            pl.BlockSpec((1, m, width), lambda bi, i: (bi, i, 0)),
            _const_spec((SUBLANES, 2 * N_STATE)),
            _const_spec((8, SUBLANES, N_STATE)),
            _const_spec((16, q * 128, 256)),
            _const_spec((q, 2, 1024, 256)),
            _const_spec((q, 2, 1024, 256)),
            _const_spec((q, 2, q * 256, 256)),
            _const_spec((1, width)),
        ],
        out_specs=[
            pl.BlockSpec((1, SSM_WIDTH // 128, q * m, 128), lambda bi, i: (bi, 0, i, 0)),
            pl.BlockSpec((1, SUBLANES, 2 * N_STATE), lambda bi, i: (bi, 0, 0)),
        ],
        out_shape=[
            jax.ShapeDtypeStruct((b, SSM_WIDTH // 128, q * rows, 128), F32),
            jax.ShapeDtypeStruct((b, SUBLANES, 2 * N_STATE), F32),
        ],
        scratch_shapes=[
            pltpu.VMEM((m, 2 * N_STATE), F32),
            pltpu.VMEM((SUBLANES, 2 * N_STATE), F32),
        ],
        compiler_params=_params(2),
    )(u4, init_state, prep["tab"], prep["wb"], prep["wcr"], prep["wci"], prep["wk"], prep["d"])


def _ssm_prep(lam_re, lam_im, log_step, b_re, b_im, c_re, c_im, d_skip):
    q = SSM_Q
    hp = lax.Precision.HIGHEST
    lr = lam_re.astype(F32)
    li = lam_im.astype(F32)
    step = jnp.exp(log_step.astype(F32))[:, None]
    mag = jnp.exp(lr * step)
    ar = mag * jnp.cos(li * step)
    ai = mag * jnp.sin(li * step)
    den = lr * lr + li * li
    cr = ((ar - 1.0) * lr + ai * li) / den
    ci = (ai * lr - (ar - 1.0) * li) / den
    br_ = b_re.astype(F32)
    bi_ = b_im.astype(F32)
    bbr = cr[..., None] * br_ - ci[..., None] * bi_
    bbi = cr[..., None] * bi_ + ci[..., None] * br_

    def powers(xr, xi, n):
        outr, outi = [jnp.ones_like(xr)], [jnp.zeros_like(xi)]
        for _ in range(n):
            pr, pi = outr[-1], outi[-1]
            outr.append(pr * xr - pi * xi)
            outi.append(pr * xi + pi * xr)
        return outr, outi

    pr, pi = powers(ar, ai, q)

    vre = jnp.stack([pr[q - 1 - s][..., None] * bbr - pi[q - 1 - s][..., None] * bbi for s in range(q)])
    vim = jnp.stack([pr[q - 1 - s][..., None] * bbi + pi[q - 1 - s][..., None] * bbr for s in range(q)])
    val = jnp.stack([vre, vim])
    val = val.reshape(2, q, 8, 4, SSM_STATE, SSM_GROUP)
    nb = jnp.arange(8)[:, None, None]
    gl = jnp.arange(8)[None, :, None]
    gq = jnp.arange(4)[None, None, :]
    mask = (gl == 4 * (nb % 2) + gq).astype(F32)
    wb = jnp.einsum("asnqpc,nlq->anslcqp", val, mask)
    wb = wb.reshape(16, q * 128, 256).astype(BF16)

    cre = c_re.astype(F32)
    cim = c_im.astype(F32)
    care = jnp.stack([cre * pr[t][:, None, :] - cim * pi[t][:, None, :] for t in range(q + 1)])
    caim = jnp.stack([cre * pi[t][:, None, :] + cim * pr[t][:, None, :] for t in range(q + 1)])
    eye = jnp.eye(16, dtype=F32)

    def pack_c(cv):
        cv = cv.reshape(q, 2, 16, SSM_GROUP, SSM_STATE)
        return jnp.einsum("rhgcp,gl->rhgplc", cv, eye).reshape(q, 2, 1024, 256).astype(BF16)

    wcr = pack_c(care[1:])
    wci = pack_c(-caim[1:])

    kt = (jnp.einsum("tgcp,gpd->tgcd", care[:q], bbr, precision=hp)
          - jnp.einsum("tgcp,gpd->tgcd", caim[:q], bbi, precision=hp))
    kt = kt.reshape(q, 2, 16, SSM_GROUP, SSM_GROUP)
    tt = jnp.einsum("thgcd,gl->thgdlc", kt, eye).reshape(q, 2, 256, 256)
    zero = jnp.zeros((2, 256, 256), F32)
    wk = jnp.stack([
        jnp.concatenate([tt[r - s] if s <= r else zero for s in range(q)], axis=1)
        for r in range(q)]).astype(BF16)

    aqr, aqi = powers(pr[q].reshape(1, N_STATE), pi[q].reshape(1, N_STATE), SUBLANES)
    rows = jnp.arange(SUBLANES)[:, None]
    tabs = []
    for d in (1, 2, 4):
        keep = (rows >= d).astype(F32)
        tabs += [keep * aqr[d], keep * aqi[d]]
    tabs += [jnp.concatenate(aqr[1:], axis=0), jnp.concatenate(aqi[1:], axis=0)]
    tab = jnp.stack(tabs)

    d4 = jnp.tile(d_skip.astype(F32).reshape(1, SSM_WIDTH), (1, q))
    return dict(wb=wb, wcr=wcr, wci=wci, wk=wk, tab=tab, d=d4)


def _mix_kernel(n_ref, y_ref, cg_ref, hist_ref, wg_ref, wglu_ref, wsp_ref,
                dww_ref, dwb_ref, lng_ref, lnb_ref, wcp_ref,
                merged_ref, hist_out_ref, buf, conv_scr, cb_scr, gate_scr, *, tt):
    hist_rows = CONV_HIST * SUBLANES

    @pl.when(pl.program_id(1) == 0)
    def _():
        buf[0:hist_rows, :] = hist_ref[...]

    @pl.when(pl.program_id(1) != 0)
    def _():
        buf[0:hist_rows, :] = buf[tt * SUBLANES:tt * SUBLANES + hist_rows, :]

    buf[hist_rows:hist_rows + tt * SUBLANES, :] = cg_ref[0]
    hist_out_ref[0] = buf[tt * SUBLANES:tt * SUBLANES + hist_rows, :]

    tb = 16
    off0 = CONV_HIST - (CONV_K - 1)
    blk = 512
    n = n_ref[0]

    def zero_of(v):
        u = pltpu.bitcast(v, jnp.uint32)
        return pltpu.bitcast((u >> 16) >> 16, F32)

    def plus(a, b):
        return b if a is None else (a if b is None else a + b)

    def conv_block(base, after):
        bias = dwb_ref[...] if after is None else dwb_ref[...] + after
        for sub in range(base, base + tb, SUBLANES):
            accs = [bias for _ in range(SUBLANES)]
            for k in range(CONV_K):
                wk = dww_ref[k]
                for t in range(SUBLANES):
                    r0 = (sub + t + off0 + k) * SUBLANES
                    accs[t] = accs[t] + buf[r0:r0 + SUBLANES, :] * wk
            for t in range(SUBLANES):
                r0 = (sub + t) * SUBLANES
                conv_scr[r0:r0 + SUBLANES, :] = accs[t]

        c = jnp.concatenate(
            [conv_scr[pl.ds(base * SUBLANES + s, tb, stride=SUBLANES), :] for s in range(SUBLANES)],
            axis=1)
        mu = jnp.mean(c, axis=-1, keepdims=True)
        cen = c - mu
        var = jnp.mean(cen * cen, axis=-1, keepdims=True)
        ln = cen * lax.rsqrt(var + LN_EPS) * lng_ref[...] + lnb_ref[...]
        cb_scr[base:base + tb, :] = (ln * jax.nn.sigmoid(ln)).astype(BF16)
        return zero_of(ln[tb - SUBLANES:tb, CONV_WIDTH - 128:CONV_WIDTH])

    n_gate = 2 * D_MODEL // blk
    n_conv = tt // tb
    gate_done = [None] * n_gate
    conv_done = [None] * n_gate
    for c in range(n_gate):
        cols = slice(blk * c, blk * (c + 1))
        after = conv_done[c - 2] if c >= 2 else None
        if after is None:
            lhs = n
        else:
            lead = n[0:16, 0:128] + jnp.concatenate([after, after], axis=0).astype(BF16)
            lhs = jnp.concatenate([lead, n[0:16, 128:]], axis=1)
            if tt > 16:
                lhs = jnp.concatenate([lhs, n[16:]], axis=0)
        g = _dot(lhs, wg_ref[:, cols])
        gate_scr[:, cols] = jax.nn.sigmoid(g)
        gate_done[c] = zero_of(g[tt - SUBLANES:tt, blk - 128:blk])
        after = plus(gate_done[c - 1], conv_done[c - 1]) if c >= 1 else None
        done = None
        for cbi in range(c * n_conv // n_gate, (c + 1) * n_conv // n_gate):
            done = plus(done, conv_block(cbi * tb, after))
        conv_done[c] = done if done is not None or c == 0 else conv_done[c - 1]

    y = jnp.concatenate([y_ref[0, cbk] for cbk in range(SSM_WIDTH // 128)], axis=1)
    yg = (y * jax.nn.sigmoid(_dot(y.astype(BF16), wglu_ref[...]))).astype(BF16)
    cb = cb_scr[...]
    for j in range(D_MODEL // blk):
        cols = slice(blk * j, blk * (j + 1))
        gcols = slice(D_MODEL + blk * j, D_MODEL + blk * (j + 1))
        ba = _dot(yg, wsp_ref[:, cols])
        bb = _dot(cb, wcp_ref[:, cols])
        merged_ref[0, :, cols] = (gate_scr[:, cols] * ba + gate_scr[:, gcols] * bb).astype(BF16)


def _mix(n, y, cg, hist, wg, wglu, wsp, dww, dwb, lng, lnb, wcp, tt):
    b, l, _ = n.shape
    kern = functools.partial(_mix_kernel, tt=tt)
    tok = lambda width: pl.BlockSpec((1, tt, width), lambda bi, i: (bi, i, 0))
    return pl.pallas_call(
        kern,
        grid=(b, l // tt),
        in_specs=[
            tok(D_MODEL),
            pl.BlockSpec((1, SSM_WIDTH // 128, tt, 128), lambda bi, i: (bi, 0, i, 0)),
            pl.BlockSpec((1, tt * SUBLANES, CONV_LANES), lambda bi, i: (bi, i, 0)),
            _const_spec((CONV_HIST * SUBLANES, CONV_LANES)),
            _const_spec((D_MODEL, 2 * D_MODEL)),
            _const_spec((SSM_WIDTH, SSM_WIDTH)),
            _const_spec((SSM_WIDTH, D_MODEL)),
            _const_spec((CONV_K, SUBLANES, CONV_LANES)),
            _const_spec((SUBLANES, CONV_LANES)),
            _const_spec((1, CONV_WIDTH)),
            _const_spec((1, CONV_WIDTH)),
            _const_spec((CONV_WIDTH, D_MODEL)),
        ],
        out_specs=[
            tok(D_MODEL),
            pl.BlockSpec((1, CONV_HIST * SUBLANES, CONV_LANES), lambda bi, i: (bi, 0, 0)),
        ],
        out_shape=[
            jax.ShapeDtypeStruct((b, l, D_MODEL), BF16),
            jax.ShapeDtypeStruct((b, CONV_HIST * SUBLANES, CONV_LANES), F32),
        ],
        scratch_shapes=[
            pltpu.VMEM(((CONV_HIST + tt) * SUBLANES, CONV_LANES), F32),
            pltpu.VMEM((tt * SUBLANES, CONV_LANES), F32),
            pltpu.VMEM((tt, CONV_WIDTH), BF16),
            pltpu.VMEM((tt, 2 * D_MODEL), F32),
        ],
        compiler_params=_params(2, MIX_VMEM_LIMIT),
    )(n, y, cg, hist, wg, wglu, wsp,
      dww.reshape(CONV_K, SUBLANES, CONV_LANES), dwb.reshape(SUBLANES, CONV_LANES), lng, lnb, wcp)


def _mix_out_kernel(x_ref, m_ref, w_ref, g_ref, g2_ref, h_ref, n2_ref):
    o = _dot(m_ref[0], w_ref[...])
    h = x_ref[0] + _rms(o, g_ref[...])
    h_ref[0] = h
    n2_ref[0] = _rms(h, g2_ref[...]).astype(BF16)


def _mix_out(x, merged, w, g, g2, tt):
    b, l, _ = x.shape
    tok = pl.BlockSpec((1, tt, D_MODEL), lambda bi, i: (bi, i, 0))
    return pl.pallas_call(
        _mix_out_kernel,
        grid=(b, l // tt),
        in_specs=[tok, tok, _const_spec((D_MODEL, D_MODEL)), _const_spec((1, D_MODEL)),
                  _const_spec((1, D_MODEL))],
        out_specs=[tok, tok],
        out_shape=[jax.ShapeDtypeStruct((b, l, D_MODEL), F32),
                   jax.ShapeDtypeStruct((b, l, D_MODEL), BF16)],
        compiler_params=_params(2),
    )(x, merged, w, g, g2)


def _ffn_kernel(h_ref, n_ref, hist_ref, wg_ref, wv_ref, dwg_ref, dwv_ref, bg_ref, bv_ref,
                wd_ref, gpost_ref, out_ref, hist_out_ref, ubuf, act_scr, carry, *, tt, nf):
    i = pl.program_id(1)
    j = pl.program_id(2)
    fb = FFN_BLOCK

    @pl.when(j == 0)
    def _():
        out_ref[0] = jnp.zeros((tt, D_MODEL), F32)

    @pl.when(i == 0)
    def _():
        carry[j] = hist_ref[j]

    ubuf[0:FFN_HIST, :] = carry[j]
    taps = [jnp.concatenate([dwg_ref[k:k + 1, :], dwv_ref[k:k + 1, :]], axis=1) for k in range(FFN_K)]
    bias = jnp.concatenate([bg_ref[...], bv_ref[...]], axis=1)

    slab = min(tt, FFN_SLAB)

    def up(r0):
        n = n_ref[0, r0:r0 + slab, :]
        ubuf[FFN_HIST + r0:FFN_HIST + r0 + slab, 0:fb] = _dot(n, wg_ref[...])
        ubuf[FFN_HIST + r0:FFN_HIST + r0 + slab, fb:2 * fb] = _dot(n, wv_ref[...])

    up(0)
    for r0 in range(0, tt, slab):
        if r0 + slab < tt:
            up(r0 + slab)
        for c0 in range(r0, r0 + slab, FFN_CHUNK):
            win = ubuf[c0:c0 + FFN_HIST + FFN_CHUNK, :]
            conv = bias + win[FFN_HIST:] * taps[FFN_K - 1]
            for k in range(FFN_K - 1):
                d = FFN_K - 1 - k
                conv = conv + pltpu.roll(win, d, 0)[FFN_HIST:] * taps[k]
            act_scr[c0:c0 + FFN_CHUNK, :] = (
                jax.nn.gelu(conv[:, 0:fb]) * conv[:, fb:2 * fb]).astype(BF16)
        out_ref[0, r0:r0 + slab, :] += _dot(act_scr[r0:r0 + slab, :], wd_ref[...])

    tail = ubuf[tt:tt + FFN_HIST, :]
    carry[j] = tail
    hist_out_ref[0, j] = tail

    @pl.when(j == nf - 1)
    def _():
        out_ref[0] = h_ref[0] + _rms(out_ref[0], gpost_ref[...])


def _ffn(h, n2, hist, wup, dww, dwb, wdown, gpost, tt):
    b, l, _ = h.shape
    fb = FFN_BLOCK
    nf = D_FF // fb
    kern = functools.partial(_ffn_kernel, tt=tt, nf=nf)
    tok = pl.BlockSpec((1, tt, D_MODEL), lambda bi, i, j: (bi, i, 0))
    return pl.pallas_call(
        kern,
        grid=(b, l // tt, nf),
        in_specs=[
            pl.BlockSpec((1, tt, D_MODEL), lambda bi, i, j: (bi, i, 0), pipeline_mode=pl.Buffered(1)),
            tok,
            _const_spec((nf, FFN_HIST, 2 * fb)),
            pl.BlockSpec((D_MODEL, fb), lambda bi, i, j: (0, j)),
            pl.BlockSpec((D_MODEL, fb), lambda bi, i, j: (0, nf + j)),
            pl.BlockSpec((FFN_K, fb), lambda bi, i, j: (0, j)),
            pl.BlockSpec((FFN_K, fb), lambda bi, i, j: (0, nf + j)),
            pl.BlockSpec((1, fb), lambda bi, i, j: (0, j)),
            pl.BlockSpec((1, fb), lambda bi, i, j: (0, nf + j)),
            pl.BlockSpec((fb, D_MODEL), lambda bi, i, j: (j, 0)),
            _const_spec((1, D_MODEL)),
        ],
        out_specs=[
            tok,
            pl.BlockSpec((1, nf, FFN_HIST, 2 * fb), lambda bi, i, j: (bi, 0, 0, 0)),
        ],
        out_shape=[
            jax.ShapeDtypeStruct((b, l, D_MODEL), F32),
            jax.ShapeDtypeStruct((b, nf, FFN_HIST, 2 * fb), F32),
        ],
        scratch_shapes=[
            pltpu.VMEM((FFN_HIST + tt, 2 * fb), F32),
            pltpu.VMEM((tt, fb), BF16),
            pltpu.VMEM((nf, FFN_HIST, 2 * fb), F32),
        ],
        compiler_params=_params(3, FFN_VMEM_LIMIT),
    )(h, n2, hist, wup, wup, dww, dww, dwb, dwb, wdown, gpost)


def _block(x, carries, p, ssm, tiles):
    b, l, _ = x.shape
    q = SSM_Q
    tt_in, m_ssm, tt_mix, tt_out, tt_ffn = tiles
    state0, conv_hist0, ffn_hist0 = carries

    n, u4, cg = _in_proj(x, p["g_mix_pre"], p["w_in_a"], tt_in)

    rows = l // q
    pad = (-rows) % m_ssm
    if pad:
        u4 = jnp.pad(u4, ((0, 0), (0, pad), (0, 0)))
    last_row = (rows - 1) % SUBLANES
    y, state = _ssm(u4, state0, ssm, m_ssm, last_row)
    if pad:
        y = y[:, :, :l]

    merged, conv_hist = _mix(n, y, cg, conv_hist0, p["w_in_g"], p["w_glu"],
                             p["w_sp"], p["conv_w"], p["conv_b"], p["ln_g"], p["ln_b"],
                             p["w_cp"], tt_mix)
    h1, n2 = _mix_out(x, merged, p["w_mo"], p["g_mix_post"], p["g_ffn_pre"], tt_out)
    out, ffn_hist = _ffn(h1, n2, ffn_hist0, p["w_up"], p["ffn_w"], p["ffn_b"],
                         p["w_down"], p["g_ffn_post"], tt_ffn)
    return out, (state[0], conv_hist[0], ffn_hist[0])


def kernel(x, meta_tokens, norm_mix_pre, w_in, lam_re, lam_im, log_step, ssm_b_re, ssm_b_im,
           ssm_c_re, ssm_c_im, ssm_d, w_ssm_glu, w_ssm_proj, conv_dw_w, conv_dw_b, conv_ln_g,
           conv_ln_b, w_conv_proj, w_mix_out, norm_mix_post, norm_ffn_pre, w_ffn_up, ffn_dw_w,
           ffn_dw_b, w_ffn_down, norm_ffn_post):
    depth = w_in.shape[0]
    n_a = SSM_WIDTH + 2 * CONV_WIDTH
    nf = D_FF // FFN_BLOCK
    zero_carries = (
        jnp.zeros((SUBLANES, 2 * N_STATE), F32),
        jnp.zeros((CONV_HIST * SUBLANES, CONV_LANES), F32),
        jnp.zeros((nf, FFN_HIST, 2 * FFN_BLOCK), F32),
    )
    h_meta = meta_tokens.astype(x.dtype)[None]
    h = x
    row = lambda v: v.astype(F32).reshape(1, -1)
    for i in range(depth):
        p = dict(
            g_mix_pre=row(norm_mix_pre[i]),
            w_in_a=w_in[i][:, :n_a].astype(BF16),
            w_in_g=w_in[i][:, n_a:].astype(BF16),
            w_glu=w_ssm_glu[i].astype(BF16),
            w_sp=w_ssm_proj[i].astype(BF16),
            conv_w=conv_dw_w[i].astype(F32),
            conv_b=row(conv_dw_b[i]),
            ln_g=row(conv_ln_g[i]),
            ln_b=row(conv_ln_b[i]),
            w_cp=w_conv_proj[i].astype(BF16),
            w_mo=w_mix_out[i].astype(BF16),
            g_mix_post=row(norm_mix_post[i]),
            g_ffn_pre=row(norm_ffn_pre[i]),
            w_up=w_ffn_up[i].astype(BF16),
            ffn_w=ffn_dw_w[i].astype(F32),
            ffn_b=row(ffn_dw_b[i]),
            w_down=w_ffn_down[i].astype(BF16),
            g_ffn_post=row(norm_ffn_post[i]),
        )
        ssm = _ssm_prep(lam_re[i], lam_im[i], log_step[i], ssm_b_re[i], ssm_b_im[i],
                        ssm_c_re[i], ssm_c_im[i], ssm_d[i])
        h_meta, carries = _block(h_meta, zero_carries, p, ssm,
                                 (N_META, SUBLANES, N_META, N_META, N_META))
        h, _ = _block(h, carries, p, ssm, (512, 512, 512, 512, 1024))
    return h
```

```python
import functools

import jax
import jax.numpy as jnp
from jax import lax
from jax.experimental import pallas as pl
from jax.experimental.pallas import tpu as pltpu

F32 = jnp.float32
BF16 = jnp.bfloat16

D_MODEL = 2048
N_META = 16
SSM_WIDTH = 512
SSM_GROUP = 16
SSM_GROUPS = 32
SSM_STATE = 64
N_STATE = SSM_GROUPS * SSM_STATE
CONV_WIDTH = 1024
CONV_K = 31
D_FF = 5632
FFN_K = 3
NORM_EPS = 1e-6
LN_EPS = 1e-5

SSM_Q = 4
CONV_HIST = 32
FFN_HIST = 8
FFN_BLOCK = 512
FFN_SLAB = 512
FFN_CHUNK = 16
SUBLANES = 8
CONV_LANES = CONV_WIDTH // SUBLANES

VMEM_LIMIT = 56 * 1024 * 1024
MIX_VMEM_LIMIT = 60 * 1024 * 1024
FFN_VMEM_LIMIT = 62 * 1024 * 1024


def _params(n_axes, vmem_limit=VMEM_LIMIT):
    return pltpu.CompilerParams(
        dimension_semantics=("arbitrary",) * n_axes,
        vmem_limit_bytes=vmem_limit)


def _const_spec(shape):
    zeros = (0,) * len(shape)
    return pl.BlockSpec(shape, lambda *_: zeros, pipeline_mode=pl.Buffered(1))


def _rms(x, g):
    ms = jnp.mean(x * x, axis=-1, keepdims=True)
    return x * lax.rsqrt(ms + NORM_EPS) * g


def _dot(a, b):
    return jnp.dot(a, b, preferred_element_type=F32)


def _in_proj_kernel(x_ref, g_ref, w_ref, n_ref, u_ref, cg_ref, u_scr, *, tt):
    q = SSM_Q
    lanes = 128
    n = _rms(x_ref[0], g_ref[...]).astype(BF16)
    n_ref[0] = n
    u = _dot(n, w_ref[:, 0:SSM_WIDTH])
    for cb in range(SSM_WIDTH // lanes):
        u_scr[cb] = u[:, lanes * cb:lanes * (cb + 1)]
    for s in range(q):
        for cb in range(SSM_WIDTH // lanes):
            c0 = SSM_WIDTH * s + lanes * cb
            u_ref[0, :, c0:c0 + lanes] = u_scr[cb, pl.ds(s, tt // q, stride=q), :]
    blk = 512
    for j in range(CONV_WIDTH // blk):
        v0 = SSM_WIDTH + blk * j
        g0 = SSM_WIDTH + CONV_WIDTH + blk * j
        val = _dot(n, w_ref[:, v0:v0 + blk])
        gate = _dot(n, w_ref[:, g0:g0 + blk])
        cg = val * jax.nn.sigmoid(gate)
        for e in range(blk // lanes):
            s = j * (blk // lanes) + e
            cg_ref[0, pl.ds(s, tt, stride=SUBLANES), :] = cg[:, lanes * e:lanes * (e + 1)]


def _in_proj(x, g, w, tt):
    b, l, _ = x.shape
    ncol = w.shape[1]
    q = SSM_Q
    return pl.pallas_call(
        functools.partial(_in_proj_kernel, tt=tt),
        grid=(b, l // tt),
        in_specs=[
            pl.BlockSpec((1, tt, D_MODEL), lambda bi, i: (bi, i, 0)),
            _const_spec((1, D_MODEL)),
            _const_spec((D_MODEL, ncol)),
        ],
        out_specs=[
            pl.BlockSpec((1, tt, D_MODEL), lambda bi, i: (bi, i, 0)),
            pl.BlockSpec((1, tt // q, q * SSM_WIDTH), lambda bi, i: (bi, i, 0)),
            pl.BlockSpec((1, tt * SUBLANES, CONV_LANES), lambda bi, i: (bi, i, 0)),
        ],
        out_shape=[
            jax.ShapeDtypeStruct((b, l, D_MODEL), BF16),
            jax.ShapeDtypeStruct((b, l // q, q * SSM_WIDTH), F32),
            jax.ShapeDtypeStruct((b, l * SUBLANES, CONV_LANES), F32),
        ],
        scratch_shapes=[pltpu.VMEM((SSM_WIDTH // 128, tt, 128), F32)],
        compiler_params=_params(2),
    )(x, g, w)


def _ssm_kernel(u_ref, init_ref, tab_ref, wb_ref, wcr_ref, wci_ref, wk_ref, d_ref,
                y_ref, state_out_ref, x_scr, state_scr, *, m, last_row):
    q = SSM_Q
    lane_blk = 128
    tile_n = 256

    @pl.when(pl.program_id(1) == 0)
    def _():
        state_scr[...] = init_ref[...]

    u = u_ref[0]
    ub = u.astype(BF16)

    for n in range(2 * N_STATE // tile_n):
        kb = (n % (N_STATE // tile_n)) // 2
        lhs = jnp.concatenate(
            [ub[:, SSM_WIDTH * s + lane_blk * kb:SSM_WIDTH * s + lane_blk * (kb + 1)]
             for s in range(q)], axis=1)
        x_scr[:, tile_n * n:tile_n * (n + 1)] = _dot(lhs, wb_ref[n])

    w = 512
    row_id = lax.broadcasted_iota(jnp.int32, (SUBLANES, w), 0)
    for j in range(N_STATE // w):
        sr = slice(w * j, w * (j + 1))
        si = slice(N_STATE + w * j, N_STATE + w * (j + 1))
        tabs = [tab_ref[t, :, sr] for t in range(8)]

        cr, ci = state_scr[:, sr], state_scr[:, si]
        for rb in range(m // SUBLANES):
            rows = slice(rb * SUBLANES, (rb + 1) * SUBLANES)
            xr = x_scr[rows, sr]
            xi = x_scr[rows, si]
            for lvl, d in enumerate((1, 2, 4)):
                ar, ai = tabs[2 * lvl], tabs[2 * lvl + 1]
                pr = pltpu.roll(xr, d, 0)
                pi = pltpu.roll(xi, d, 0)
                xr, xi = xr + ar * pr - ai * pi, xi + ar * pi + ai * pr
            ar, ai = tabs[6], tabs[7]
            xr, xi = xr + ar * cr - ai * ci, xi + ar * ci + ai * cr
            x_scr[rows, sr] = jnp.where(row_id == 0, cr, pltpu.roll(xr, 1, 0))
            x_scr[rows, si] = jnp.where(row_id == 0, ci, pltpu.roll(xi, 1, 0))
            if rb == m // SUBLANES - 1:
                state_scr[:, sr] = jnp.broadcast_to(xr[last_row:last_row + 1], (SUBLANES, w))
                state_scr[:, si] = jnp.broadcast_to(xi[last_row:last_row + 1], (SUBLANES, w))
            cr = jnp.broadcast_to(xr[SUBLANES - 1:SUBLANES], (SUBLANES, w))
            ci = jnp.broadcast_to(xi[SUBLANES - 1:SUBLANES], (SUBLANES, w))

    state_out_ref[0] = state_scr[...]

    half = 256
    st_per_half = half // SSM_GROUP * SSM_STATE
    for h in range(SSM_WIDTH // half):
        er = x_scr[:, st_per_half * h:st_per_half * (h + 1)].astype(BF16)
        ei = x_scr[:, N_STATE + st_per_half * h:N_STATE + st_per_half * (h + 1)].astype(BF16)
        for r in range(q):
            lanes = slice(SSM_WIDTH * r + half * h, SSM_WIDTH * r + half * (h + 1))
            lhs = jnp.concatenate(
                [ub[:, SSM_WIDTH * s + half * h:SSM_WIDTH * s + half * (h + 1)]
                 for s in range(r + 1)], axis=1)
            acc = _dot(er, wcr_ref[r, h]) + _dot(ei, wci_ref[r, h])
            acc = acc + _dot(lhs, wk_ref[r, h, 0:half * (r + 1), :])
            y = jax.nn.gelu(acc + d_ref[:, lanes] * u[:, lanes])
            for e in range(half // lane_blk):
                y_ref[0, (half // lane_blk) * h + e, pl.ds(r, m, stride=q), :] = (
                    y[:, lane_blk * e:lane_blk * (e + 1)])


def _ssm(u4, init_state, prep, m, last_row):
    b, rows, width = u4.shape
    q = SSM_Q
    kern = functools.partial(_ssm_kernel, m=m, last_row=last_row)
    return pl.pallas_call(
        kern,
        grid=(b, rows // m),
        in_specs=[
            pl.BlockSpec((1, m, width), lambda bi, i: (bi, i, 0)),
            _const_spec((SUBLANES, 2 * N_STATE)),
            _const_spec((8, SUBLANES, N_STATE)),
            _const_spec((16, q * 128, 256)),
            _const_spec((q, 2, 1024, 256)),
            _const_spec((q, 2, 1024, 256)),
            _const_spec((q, 2, q * 256, 256)),
            _const_spec((1, width)),
        ],
        out_specs=[
            pl.BlockSpec((1, SSM_WIDTH // 128, q * m, 128), lambda bi, i: (bi, 0, i, 0)),
            pl.BlockSpec((1, SUBLANES, 2 * N_STATE), lambda bi, i: (bi, 0, 0)),
        ],
        out_shape=[
            jax.ShapeDtypeStruct((b, SSM_WIDTH // 128, q * rows, 128), F32),
            jax.ShapeDtypeStruct((b, SUBLANES, 2 * N_STATE), F32),
        ],
        scratch_shapes=[
            pltpu.VMEM((m, 2 * N_STATE), F32),
            pltpu.VMEM((SUBLANES, 2 * N_STATE), F32),
        ],
        compiler_params=_params(2),
    )(u4, init_state, prep["tab"], prep["wb"], prep["wcr"], prep["wci"], prep["wk"], prep["d"])


def _ssm_prep(lam_re, lam_im, log_step, b_re, b_im, c_re, c_im, d_skip):
    q = SSM_Q
    hp = lax.Precision.HIGHEST
    lr = lam_re.astype(F32)
    li = lam_im.astype(F32)
    step = jnp.exp(log_step.astype(F32))[:, None]
    mag = jnp.exp(lr * step)
    ar = mag * jnp.cos(li * step)
    ai = mag * jnp.sin(li * step)
    den = lr * lr + li * li
    cr = ((ar - 1.0) * lr + ai * li) / den
    ci = (ai * lr - (ar - 1.0) * li) / den
    br_ = b_re.astype(F32)
    bi_ = b_im.astype(F32)
    bbr = cr[..., None] * br_ - ci[..., None] * bi_
    bbi = cr[..., None] * bi_ + ci[..., None] * br_

    def powers(xr, xi, n):
        outr, outi = [jnp.ones_like(xr)], [jnp.zeros_like(xi)]
        for _ in range(n):
            pr, pi = outr[-1], outi[-1]
            outr.append(pr * xr - pi * xi)
            outi.append(pr * xi + pi * xr)
        return outr, outi

    pr, pi = powers(ar, ai, q)

    vre = jnp.stack([pr[q - 1 - s][..., None] * bbr - pi[q - 1 - s][..., None] * bbi for s in range(q)])
    vim = jnp.stack([pr[q - 1 - s][..., None] * bbi + pi[q - 1 - s][..., None] * bbr for s in range(q)])
    val = jnp.stack([vre, vim])
    val = val.reshape(2, q, 8, 4, SSM_STATE, SSM_GROUP)
    nb = jnp.arange(8)[:, None, None]
    gl = jnp.arange(8)[None, :, None]
    gq = jnp.arange(4)[None, None, :]
    mask = (gl == 4 * (nb % 2) + gq).astype(F32)
    wb = jnp.einsum("asnqpc,nlq->anslcqp", val, mask)
    wb = wb.reshape(16, q * 128, 256).astype(BF16)

    cre = c_re.astype(F32)
    cim = c_im.astype(F32)
    care = jnp.stack([cre * pr[t][:, None, :] - cim * pi[t][:, None, :] for t in range(q + 1)])
    caim = jnp.stack([cre * pi[t][:, None, :] + cim * pr[t][:, None, :] for t in range(q + 1)])
    eye = jnp.eye(16, dtype=F32)

    def pack_c(cv):
        cv = cv.reshape(q, 2, 16, SSM_GROUP, SSM_STATE)
        return jnp.einsum("rhgcp,gl->rhgplc", cv, eye).reshape(q, 2, 1024, 256).astype(BF16)

    wcr = pack_c(care[1:])
    wci = pack_c(-caim[1:])

    kt = (jnp.einsum("tgcp,gpd->tgcd", care[:q], bbr, precision=hp)
          - jnp.einsum("tgcp,gpd->tgcd", caim[:q], bbi, precision=hp))
    kt = kt.reshape(q, 2, 16, SSM_GROUP, SSM_GROUP)
    tt = jnp.einsum("thgcd,gl->thgdlc", kt, eye).reshape(q, 2, 256, 256)
    zero = jnp.zeros((2, 256, 256), F32)
    wk = jnp.stack([
        jnp.concatenate([tt[r - s] if s <= r else zero for s in range(q)], axis=1)
        for r in range(q)]).astype(BF16)

    aqr, aqi = powers(pr[q].reshape(1, N_STATE), pi[q].reshape(1, N_STATE), SUBLANES)
    rows = jnp.arange(SUBLANES)[:, None]
    tabs = []
    for d in (1, 2, 4):
        keep = (rows >= d).astype(F32)
        tabs += [keep * aqr[d], keep * aqi[d]]
    tabs += [jnp.concatenate(aqr[1:], axis=0), jnp.concatenate(aqi[1:], axis=0)]
    tab = jnp.stack(tabs)

    d4 = jnp.tile(d_skip.astype(F32).reshape(1, SSM_WIDTH), (1, q))
    return dict(wb=wb, wcr=wcr, wci=wci, wk=wk, tab=tab, d=d4)


def _mix_kernel(n_ref, y_ref, cg_ref, hist_ref, wg_ref, wglu_ref, wsp_ref,
                dww_ref, dwb_ref, lng_ref, lnb_ref, wcp_ref,
                merged_ref, hist_out_ref, buf, conv_scr, cb_scr, gate_scr, *, tt):
    hist_rows = CONV_HIST * SUBLANES

    @pl.when(pl.program_id(1) == 0)
    def _():
        buf[0:hist_rows, :] = hist_ref[...]

    @pl.when(pl.program_id(1) != 0)
    def _():
        buf[0:hist_rows, :] = buf[tt * SUBLANES:tt * SUBLANES + hist_rows, :]

    buf[hist_rows:hist_rows + tt * SUBLANES, :] = cg_ref[0]
    hist_out_ref[0] = buf[tt * SUBLANES:tt * SUBLANES + hist_rows, :]

    tb = 16
    off0 = CONV_HIST - (CONV_K - 1)
    blk = 512
    n = n_ref[0]

    def zero_of(v):
        u = pltpu.bitcast(v, jnp.uint32)
        return pltpu.bitcast((u >> 16) >> 16, F32)

    def plus(a, b):
        return b if a is None else (a if b is None else a + b)

    def conv_block(base, after):
        bias = dwb_ref[...] if after is None else dwb_ref[...] + after
        for sub in range(base, base + tb, SUBLANES):
            accs = [bias for _ in range(SUBLANES)]
            for k in range(CONV_K):
                wk = dww_ref[k]
                for t in range(SUBLANES):
                    r0 = (sub + t + off0 + k) * SUBLANES
                    accs[t] = accs[t] + buf[r0:r0 + SUBLANES, :] * wk
            for t in range(SUBLANES):
                r0 = (sub + t) * SUBLANES
                conv_scr[r0:r0 + SUBLANES, :] = accs[t]

        c = jnp.concatenate(
            [conv_scr[pl.ds(base * SUBLANES + s, tb, stride=SUBLANES), :] for s in range(SUBLANES)],
            axis=1)
        mu = jnp.mean(c, axis=-1, keepdims=True)
        cen = c - mu
        var = jnp.mean(cen * cen, axis=-1, keepdims=True)
        ln = cen * lax.rsqrt(var + LN_EPS) * lng_ref[...] + lnb_ref[...]
        cb_scr[base:base + tb, :] = (ln * jax.nn.sigmoid(ln)).astype(BF16)
        return zero_of(ln[tb - SUBLANES:tb, CONV_WIDTH - 128:CONV_WIDTH])

    n_gate = 2 * D_MODEL // blk
    n_conv = tt // tb
    gate_done = [None] * n_gate
    conv_done = [None] * n_gate
    for c in range(n_gate):
        cols = slice(blk * c, blk * (c + 1))
        after = conv_done[c - 2] if c >= 2 else None
        if after is None:
            lhs = n
        else:
            lead = n[0:16, 0:128] + jnp.concatenate([after, after], axis=0).astype(BF16)
            lhs = jnp.concatenate([lead, n[0:16, 128:]], axis=1)
            if tt > 16:
                lhs = jnp.concatenate([lhs, n[16:]], axis=0)
        g = _dot(lhs, wg_ref[:, cols])
        gate_scr[:, cols] = jax.nn.sigmoid(g)
        gate_done[c] = zero_of(g[tt - SUBLANES:tt, blk - 128:blk])
        after = plus(gate_done[c - 1], conv_done[c - 1]) if c >= 1 else None
        done = None
        for cbi in range(c * n_conv // n_gate, (c + 1) * n_conv // n_gate):
            done = plus(done, conv_block(cbi * tb, after))
        conv_done[c] = done if done is not None or c == 0 else conv_done[c - 1]

    y = jnp.concatenate([y_ref[0, cbk] for cbk in range(SSM_WIDTH // 128)], axis=1)
    yg = (y * jax.nn.sigmoid(_dot(y.astype(BF16), wglu_ref[...]))).astype(BF16)
    cb = cb_scr[...]
    for j in range(D_MODEL // blk):
        cols = slice(blk * j, blk * (j + 1))
        gcols = slice(D_MODEL + blk * j, D_MODEL + blk * (j + 1))
        ba = _dot(yg, wsp_ref[:, cols])
        bb = _dot(cb, wcp_ref[:, cols])
        merged_ref[0, :, cols] = (gate_scr[:, cols] * ba + gate_scr[:, gcols] * bb).astype(BF16)


def _mix(n, y, cg, hist, wg, wglu, wsp, dww, dwb, lng, lnb, wcp, tt):
    b, l, _ = n.shape
    kern = functools.partial(_mix_kernel, tt=tt)
    tok = lambda width: pl.BlockSpec((1, tt, width), lambda bi, i: (bi, i, 0))
    return pl.pallas_call(
        kern,
        grid=(b, l // tt),
        in_specs=[
            tok(D_MODEL),
            pl.BlockSpec((1, SSM_WIDTH // 128, tt, 128), lambda bi, i: (bi, 0, i, 0)),
            pl.BlockSpec((1, tt * SUBLANES, CONV_LANES), lambda bi, i: (bi, i, 0)),
            _const_spec((CONV_HIST * SUBLANES, CONV_LANES)),
            _const_spec((D_MODEL, 2 * D_MODEL)),
            _const_spec((SSM_WIDTH, SSM_WIDTH)),
            _const_spec((SSM_WIDTH, D_MODEL)),
            _const_spec((CONV_K, SUBLANES, CONV_LANES)),
            _const_spec((SUBLANES, CONV_LANES)),
            _const_spec((1, CONV_WIDTH)),
            _const_spec((1, CONV_WIDTH)),
            _const_spec((CONV_WIDTH, D_MODEL)),
        ],
        out_specs=[
            tok(D_MODEL),
            pl.BlockSpec((1, CONV_HIST * SUBLANES, CONV_LANES), lambda bi, i: (bi, 0, 0)),
        ],
        out_shape=[
            jax.ShapeDtypeStruct((b, l, D_MODEL), BF16),
            jax.ShapeDtypeStruct((b, CONV_HIST * SUBLANES, CONV_LANES), F32),
        ],
        scratch_shapes=[
            pltpu.VMEM(((CONV_HIST + tt) * SUBLANES, CONV_LANES), F32),
            pltpu.VMEM((tt * SUBLANES, CONV_LANES), F32),
            pltpu.VMEM((tt, CONV_WIDTH), BF16),
            pltpu.VMEM((tt, 2 * D_MODEL), F32),
        ],
        compiler_params=_params(2, MIX_VMEM_LIMIT),
    )(n, y, cg, hist, wg, wglu, wsp,
      dww.reshape(CONV_K, SUBLANES, CONV_LANES), dwb.reshape(SUBLANES, CONV_LANES), lng, lnb, wcp)


def _mix_out_kernel(x_ref, m_ref, w_ref, g_ref, g2_ref, h_ref, n2_ref):
    o = _dot(m_ref[0], w_ref[...])
    h = x_ref[0] + _rms(o, g_ref[...])
    h_ref[0] = h
    n2_ref[0] = _rms(h, g2_ref[...]).astype(BF16)


def _mix_out(x, merged, w, g, g2, tt):
    b, l, _ = x.shape
    tok = pl.BlockSpec((1, tt, D_MODEL), lambda bi, i: (bi, i, 0))
    return pl.pallas_call(
        _mix_out_kernel,
        grid=(b, l // tt),
        in_specs=[tok, tok, _const_spec((D_MODEL, D_MODEL)), _const_spec((1, D_MODEL)),
                  _const_spec((1, D_MODEL))],
        out_specs=[tok, tok],
        out_shape=[jax.ShapeDtypeStruct((b, l, D_MODEL), F32),
                   jax.ShapeDtypeStruct((b, l, D_MODEL), BF16)],
        compiler_params=_params(2),
    )(x, merged, w, g, g2)


def _ffn_kernel(h_ref, n_ref, hist_ref, wg_ref, wv_ref, dw_ref, b_ref,
                wd_ref, gpost_ref, out_ref, hist_out_ref, ubuf, act_scr, carry, *, tt, nf):
    i = pl.program_id(1)
    j = pl.program_id(2)
    fb = FFN_BLOCK

    @pl.when(j == 0)
    def _():
        out_ref[0] = jnp.zeros((tt, D_MODEL), F32)

    @pl.when(i == 0)
    def _():
        carry[j] = hist_ref[j]

    ubuf[0:FFN_HIST, :] = carry[j]
    dwg, dwv = dw_ref[j], dw_ref[nf + j]
    taps = [jnp.concatenate([dwg[k:k + 1, :], dwv[k:k + 1, :]], axis=1) for k in range(FFN_K)]
    bias = jnp.concatenate([b_ref[j], b_ref[nf + j]], axis=1)

    slab = min(tt, FFN_SLAB)

    def up(r0):
        n = n_ref[0, r0:r0 + slab, :]
        ubuf[FFN_HIST + r0:FFN_HIST + r0 + slab, 0:fb] = _dot(n, wg_ref[0])
        ubuf[FFN_HIST + r0:FFN_HIST + r0 + slab, fb:2 * fb] = _dot(n, wv_ref[0])

    up(0)
    for r0 in range(0, tt, slab):
        if r0 + slab < tt:
            up(r0 + slab)
        for c0 in range(r0, r0 + slab, FFN_CHUNK):
            win = ubuf[c0:c0 + FFN_HIST + FFN_CHUNK, :]
            conv = bias + win[FFN_HIST:] * taps[FFN_K - 1]
            for k in range(FFN_K - 1):
                d = FFN_K - 1 - k
                conv = conv + pltpu.roll(win, d, 0)[FFN_HIST:] * taps[k]
            act_scr[c0:c0 + FFN_CHUNK, :] = (
                jax.nn.gelu(conv[:, 0:fb]) * conv[:, fb:2 * fb]).astype(BF16)
        out_ref[0, r0:r0 + slab, :] += _dot(act_scr[r0:r0 + slab, :], wd_ref[...])

    tail = ubuf[tt:tt + FFN_HIST, :]
    carry[j] = tail
    hist_out_ref[0, j] = tail

    @pl.when(j == nf - 1)
    def _():
        out_ref[0] = h_ref[0] + _rms(out_ref[0], gpost_ref[...])


def _ffn(h, n2, hist, wup, dww, dwb, wdown, gpost, tt):
    b, l, _ = h.shape
    fb = FFN_BLOCK
    nf = D_FF // fb
    kern = functools.partial(_ffn_kernel, tt=tt, nf=nf)
    tok = pl.BlockSpec((1, tt, D_MODEL), lambda bi, i, j: (bi, i, 0))
    return pl.pallas_call(
        kern,
        grid=(b, l // tt, nf),
        in_specs=[
            pl.BlockSpec((1, tt, D_MODEL), lambda bi, i, j: (bi, i, 0), pipeline_mode=pl.Buffered(1)),
            tok,
            _const_spec((nf, FFN_HIST, 2 * fb)),
            pl.BlockSpec((1, D_MODEL, fb), lambda bi, i, j: (j, 0, 0)),
            pl.BlockSpec((1, D_MODEL, fb), lambda bi, i, j: (nf + j, 0, 0)),
            _const_spec((2 * nf, FFN_K, fb)),
            _const_spec((2 * nf, 1, fb)),
            pl.BlockSpec((fb, D_MODEL), lambda bi, i, j: (j, 0)),
            _const_spec((1, D_MODEL)),
        ],
        out_specs=[
            tok,
            pl.BlockSpec((1, nf, FFN_HIST, 2 * fb), lambda bi, i, j: (bi, 0, 0, 0)),
        ],
        out_shape=[
            jax.ShapeDtypeStruct((b, l, D_MODEL), F32),
            jax.ShapeDtypeStruct((b, nf, FFN_HIST, 2 * fb), F32),
        ],
        scratch_shapes=[
            pltpu.VMEM((FFN_HIST + tt, 2 * fb), F32),
            pltpu.VMEM((tt, fb), BF16),
            pltpu.VMEM((nf, FFN_HIST, 2 * fb), F32),
        ],
        compiler_params=_params(3, FFN_VMEM_LIMIT),
    )(h, n2, hist, wup, wup, dww, dwb, wdown, gpost)


def _block(x, carries, p, ssm, tiles):
    b, l, _ = x.shape
    q = SSM_Q
    tt_in, m_ssm, tt_mix, tt_out, tt_ffn = tiles
    state0, conv_hist0, ffn_hist0 = carries

    n, u4, cg = _in_proj(x, p["g_mix_pre"], p["w_in_a"], tt_in)

    rows = l // q
    pad = (-rows) % m_ssm
    if pad:
        u4 = jnp.pad(u4, ((0, 0), (0, pad), (0, 0)))
    last_row = (rows - 1) % SUBLANES
    y, state = _ssm(u4, state0, ssm, m_ssm, last_row)
    if pad:
        y = y[:, :, :l]

    merged, conv_hist = _mix(n, y, cg, conv_hist0, p["w_in_g"], p["w_glu"],
                             p["w_sp"], p["conv_w"], p["conv_b"], p["ln_g"], p["ln_b"],
                             p["w_cp"], tt_mix)
    h1, n2 = _mix_out(x, merged, p["w_mo"], p["g_mix_post"], p["g_ffn_pre"], tt_out)
    out, ffn_hist = _ffn(h1, n2, ffn_hist0, p["w_up"], p["ffn_w"], p["ffn_b"],
                         p["w_down"], p["g_ffn_post"], tt_ffn)
    return out, (state[0], conv_hist[0], ffn_hist[0])


def kernel(x, meta_tokens, norm_mix_pre, w_in, lam_re, lam_im, log_step, ssm_b_re, ssm_b_im,
           ssm_c_re, ssm_c_im, ssm_d, w_ssm_glu, w_ssm_proj, conv_dw_w, conv_dw_b, conv_ln_g,
           conv_ln_b, w_conv_proj, w_mix_out, norm_mix_post, norm_ffn_pre, w_ffn_up, ffn_dw_w,
           ffn_dw_b, w_ffn_down, norm_ffn_post):
    depth = w_in.shape[0]
    n_a = SSM_WIDTH + 2 * CONV_WIDTH
    nf = D_FF // FFN_BLOCK
    zero_carries = (
        jnp.zeros((SUBLANES, 2 * N_STATE), F32),
        jnp.zeros((CONV_HIST * SUBLANES, CONV_LANES), F32),
        jnp.zeros((nf, FFN_HIST, 2 * FFN_BLOCK), F32),
    )
    h_meta = meta_tokens.astype(x.dtype)[None]
    h = x
    row = lambda v: v.astype(F32).reshape(1, -1)
    blocked = lambda v: v.reshape(v.shape[0], 2 * nf, FFN_BLOCK).transpose(1, 0, 2)
    for i in range(depth):
        p = dict(
            g_mix_pre=row(norm_mix_pre[i]),
            w_in_a=w_in[i][:, :n_a].astype(BF16),
            w_in_g=w_in[i][:, n_a:].astype(BF16),
            w_glu=w_ssm_glu[i].astype(BF16),
            w_sp=w_ssm_proj[i].astype(BF16),
            conv_w=conv_dw_w[i].astype(F32),
            conv_b=row(conv_dw_b[i]),
            ln_g=row(conv_ln_g[i]),
            ln_b=row(conv_ln_b[i]),
            w_cp=w_conv_proj[i].astype(BF16),
            w_mo=w_mix_out[i].astype(BF16),
            g_mix_post=row(norm_mix_post[i]),
            g_ffn_pre=row(norm_ffn_pre[i]),
            w_up=blocked(w_ffn_up[i].astype(BF16)),
            ffn_w=blocked(ffn_dw_w[i].astype(F32)),
            ffn_b=blocked(row(ffn_dw_b[i])),
            w_down=w_ffn_down[i].astype(BF16),
            g_ffn_post=row(norm_ffn_post[i]),
        )
        ssm = _ssm_prep(lam_re[i], lam_im[i], log_step[i], ssm_b_re[i], ssm_b_im[i],
                        ssm_c_re[i], ssm_c_im[i], ssm_d[i])
        h_meta, carries = _block(h_meta, zero_carries, p, ssm,
                                 (N_META, SUBLANES, N_META, N_META, N_META))
        h, _ = _block(h, carries, p, ssm, (512, 512, 512, 512, 1024))
    return h
```

```python
import functools

import jax
import jax.numpy as jnp
from jax import lax
from jax.experimental import pallas as pl
from jax.experimental.pallas import tpu as pltpu

F32 = jnp.float32
BF16 = jnp.bfloat16

D_MODEL = 2048
N_META = 16
SSM_WIDTH = 512
SSM_GROUP = 16
SSM_GROUPS = 32
SSM_STATE = 64
N_STATE = SSM_GROUPS * SSM_STATE
CONV_WIDTH = 1024
CONV_K = 31
D_FF = 5632
FFN_K = 3
NORM_EPS = 1e-6
LN_EPS = 1e-5

SSM_Q = 4
CONV_HIST = 32
FFN_HIST = 8
MIX_BLOCK = 512
FFN_BLOCK = 512
FFN_SLAB = 512
FFN_CHUNK = 16
SUBLANES = 8
CONV_LANES = CONV_WIDTH // SUBLANES

VMEM_LIMIT = 56 * 1024 * 1024
MIX_VMEM_LIMIT = 60 * 1024 * 1024
FFN_VMEM_LIMIT = 62 * 1024 * 1024


def _params(n_axes, vmem_limit=VMEM_LIMIT):
    return pltpu.CompilerParams(
        dimension_semantics=("arbitrary",) * n_axes,
        vmem_limit_bytes=vmem_limit)


def _const_spec(shape):
    zeros = (0,) * len(shape)
    return pl.BlockSpec(shape, lambda *_: zeros, pipeline_mode=pl.Buffered(1))


def _rms(x, g):
    ms = jnp.mean(x * x, axis=-1, keepdims=True)
    return x * lax.rsqrt(ms + NORM_EPS) * g


def _dot(a, b):
    return jnp.dot(a, b, preferred_element_type=F32)


def _in_proj_kernel(x_ref, g_ref, w_ref, n_ref, u_ref, cg_ref, u_scr, *, tt):
    q = SSM_Q
    lanes = 128
    n = _rms(x_ref[0], g_ref[...]).astype(BF16)
    n_ref[0] = n
    u = _dot(n, w_ref[:, 0:SSM_WIDTH])
    for cb in range(SSM_WIDTH // lanes):
        u_scr[cb] = u[:, lanes * cb:lanes * (cb + 1)]
    for s in range(q):
        for cb in range(SSM_WIDTH // lanes):
            c0 = SSM_WIDTH * s + lanes * cb
            u_ref[0, :, c0:c0 + lanes] = u_scr[cb, pl.ds(s, tt // q, stride=q), :]
    blk = 512
    for j in range(CONV_WIDTH // blk):
        v0 = SSM_WIDTH + blk * j
        g0 = SSM_WIDTH + CONV_WIDTH + blk * j
        val = _dot(n, w_ref[:, v0:v0 + blk])
        gate = _dot(n, w_ref[:, g0:g0 + blk])
        cg = val * jax.nn.sigmoid(gate)
        for e in range(blk // lanes):
            s = j * (blk // lanes) + e
            cg_ref[0, pl.ds(s, tt, stride=SUBLANES), :] = cg[:, lanes * e:lanes * (e + 1)]


def _in_proj(x, g, w, tt):
    b, l, _ = x.shape
    ncol = SSM_WIDTH + 2 * CONV_WIDTH
    q = SSM_Q
    return pl.pallas_call(
        functools.partial(_in_proj_kernel, tt=tt),
        grid=(b, l // tt),
        in_specs=[
            pl.BlockSpec((1, tt, D_MODEL), lambda bi, i: (bi, i, 0)),
            _const_spec((1, D_MODEL)),
            _const_spec((D_MODEL, ncol)),
        ],
        out_specs=[
            pl.BlockSpec((1, tt, D_MODEL), lambda bi, i: (bi, i, 0)),
            pl.BlockSpec((1, tt // q, q * SSM_WIDTH), lambda bi, i: (bi, i, 0)),
            pl.BlockSpec((1, tt * SUBLANES, CONV_LANES), lambda bi, i: (bi, i, 0)),
        ],
        out_shape=[
            jax.ShapeDtypeStruct((b, l, D_MODEL), BF16),
            jax.ShapeDtypeStruct((b, l // q, q * SSM_WIDTH), F32),
            jax.ShapeDtypeStruct((b, l * SUBLANES, CONV_LANES), F32),
        ],
        scratch_shapes=[pltpu.VMEM((SSM_WIDTH // 128, tt, 128), F32)],
        compiler_params=_params(2),
    )(x, g, w)


def _ssm_kernel(u_ref, init_ref, tab_ref, wb_ref, wcr_ref, wci_ref, wk_ref, d_ref,
                y_ref, state_out_ref, x_scr, state_scr, *, m, last_row):
    q = SSM_Q
    lane_blk = 128
    tile_n = 256

    @pl.when(pl.program_id(1) == 0)
    def _():
        state_scr[...] = init_ref[...]

    u = u_ref[0]
    ub = u.astype(BF16)

    for n in range(2 * N_STATE // tile_n):
        kb = (n % (N_STATE // tile_n)) // 2
        lhs = jnp.concatenate(
            [ub[:, SSM_WIDTH * s + lane_blk * kb:SSM_WIDTH * s + lane_blk * (kb + 1)]
             for s in range(q)], axis=1)
        x_scr[:, tile_n * n:tile_n * (n + 1)] = _dot(lhs, wb_ref[n])

    w = 512
    row_id = lax.broadcasted_iota(jnp.int32, (SUBLANES, w), 0)
    for j in range(N_STATE // w):
        sr = slice(w * j, w * (j + 1))
        si = slice(N_STATE + w * j, N_STATE + w * (j + 1))
        tabs = [tab_ref[t, :, sr] for t in range(8)]

        cr, ci = state_scr[:, sr], state_scr[:, si]
        for rb in range(m // SUBLANES):
            rows = slice(rb * SUBLANES, (rb + 1) * SUBLANES)
            xr = x_scr[rows, sr]
            xi = x_scr[rows, si]
            for lvl, d in enumerate((1, 2, 4)):
                ar, ai = tabs[2 * lvl], tabs[2 * lvl + 1]
                pr = pltpu.roll(xr, d, 0)
                pi = pltpu.roll(xi, d, 0)
                xr, xi = xr + ar * pr - ai * pi, xi + ar * pi + ai * pr
            ar, ai = tabs[6], tabs[7]
            xr, xi = xr + ar * cr - ai * ci, xi + ar * ci + ai * cr
            x_scr[rows, sr] = jnp.where(row_id == 0, cr, pltpu.roll(xr, 1, 0))
            x_scr[rows, si] = jnp.where(row_id == 0, ci, pltpu.roll(xi, 1, 0))
            if rb == m // SUBLANES - 1:
                state_scr[:, sr] = jnp.broadcast_to(xr[last_row:last_row + 1], (SUBLANES, w))
                state_scr[:, si] = jnp.broadcast_to(xi[last_row:last_row + 1], (SUBLANES, w))
            cr = jnp.broadcast_to(xr[SUBLANES - 1:SUBLANES], (SUBLANES, w))
            ci = jnp.broadcast_to(xi[SUBLANES - 1:SUBLANES], (SUBLANES, w))

    state_out_ref[0] = state_scr[...]

    half = 256
    st_per_half = half // SSM_GROUP * SSM_STATE
    for h in range(SSM_WIDTH // half):
        er = x_scr[:, st_per_half * h:st_per_half * (h + 1)].astype(BF16)
        ei = x_scr[:, N_STATE + st_per_half * h:N_STATE + st_per_half * (h + 1)].astype(BF16)
        for r in range(q):
            lanes = slice(SSM_WIDTH * r + half * h, SSM_WIDTH * r + half * (h + 1))
            lhs = jnp.concatenate(
                [ub[:, SSM_WIDTH * s + half * h:SSM_WIDTH * s + half * (h + 1)]
                 for s in range(r + 1)], axis=1)
            acc = _dot(er, wcr_ref[r, h]) + _dot(ei, wci_ref[r, h])
            acc = acc + _dot(lhs, wk_ref[r, h, 0:half * (r + 1), :])
            y = jax.nn.gelu(acc + d_ref[:, lanes] * u[:, lanes])
            for e in range(half // lane_blk):
                y_ref[0, (half // lane_blk) * h + e, pl.ds(r, m, stride=q), :] = (
                    y[:, lane_blk * e:lane_blk * (e + 1)])


def _ssm(u4, init_state, prep, m, last_row):
    b, rows, width = u4.shape
    q = SSM_Q
    kern = functools.partial(_ssm_kernel, m=m, last_row=last_row)
    return pl.pallas_call(
        kern,
        grid=(b, rows // m),
        in_specs=[
            pl.BlockSpec((1, m, width), lambda bi, i: (bi, i, 0)),
            _const_spec((SUBLANES, 2 * N_STATE)),
            _const_spec((8, SUBLANES, N_STATE)),
            _const_spec((16, q * 128, 256)),
            _const_spec((q, 2, 1024, 256)),
            _const_spec((q, 2, 1024, 256)),
            _const_spec((q, 2, q * 256, 256)),
            _const_spec((1, width)),
        ],
        out_specs=[
            pl.BlockSpec((1, SSM_WIDTH // 128, q * m, 128), lambda bi, i: (bi, 0, i, 0)),
            pl.BlockSpec((1, SUBLANES, 2 * N_STATE), lambda bi, i: (bi, 0, 0)),
        ],
        out_shape=[
            jax.ShapeDtypeStruct((b, SSM_WIDTH // 128, q * rows, 128), F32),
            jax.ShapeDtypeStruct((b, SUBLANES, 2 * N_STATE), F32),
        ],
        scratch_shapes=[
            pltpu.VMEM((m, 2 * N_STATE), F32),
            pltpu.VMEM((SUBLANES, 2 * N_STATE), F32),
        ],
        compiler_params=_params(2),
    )(u4, init_state, prep["tab"], prep["wb"], prep["wcr"], prep["wci"], prep["wk"], prep["d"])


def _ssm_prep(lam_re, lam_im, log_step, b_re, b_im, c_re, c_im, d_skip):
    q = SSM_Q
    hp = lax.Precision.HIGHEST
    lr = lam_re.astype(F32)
    li = lam_im.astype(F32)
    step = jnp.exp(log_step.astype(F32))[:, None]
    mag = jnp.exp(lr * step)
    ar = mag * jnp.cos(li * step)
    ai = mag * jnp.sin(li * step)
    den = lr * lr + li * li
    cr = ((ar - 1.0) * lr + ai * li) / den
    ci = (ai * lr - (ar - 1.0) * li) / den
    br_ = b_re.astype(F32)
    bi_ = b_im.astype(F32)
    bbr = cr[..., None] * br_ - ci[..., None] * bi_
    bbi = cr[..., None] * bi_ + ci[..., None] * br_

    def powers(xr, xi, n):
        outr, outi = [jnp.ones_like(xr)], [jnp.zeros_like(xi)]
        for _ in range(n):
            pr, pi = outr[-1], outi[-1]
            outr.append(pr * xr - pi * xi)
            outi.append(pr * xi + pi * xr)
        return outr, outi

    pr, pi = powers(ar, ai, q)

    vre = jnp.stack([pr[q - 1 - s][..., None] * bbr - pi[q - 1 - s][..., None] * bbi for s in range(q)])
    vim = jnp.stack([pr[q - 1 - s][..., None] * bbi + pi[q - 1 - s][..., None] * bbr for s in range(q)])
    val = jnp.stack([vre, vim])
    val = val.reshape(2, q, 8, 4, SSM_STATE, SSM_GROUP).transpose(0, 2, 1, 5, 3, 4)
    val = val.reshape(2, 8, q * SSM_GROUP, 256)
    rep_rows = jnp.kron(jnp.eye(q, dtype=F32), jnp.tile(jnp.eye(SSM_GROUP, dtype=F32), (8, 1)))
    wb = jnp.einsum("Rk,anke->anRe", rep_rows, val, precision=hp)
    wb_shape = (8, q * 128, 256)
    row_gl = (lax.broadcasted_iota(jnp.int32, wb_shape, 1) // SSM_GROUP) % 8
    col_gq = lax.broadcasted_iota(jnp.int32, wb_shape, 2) // SSM_STATE
    tile_nb = lax.broadcasted_iota(jnp.int32, wb_shape, 0)
    wb = jnp.where(row_gl == 4 * (tile_nb % 2) + col_gq, wb, 0.0)
    wb = wb.reshape(16, q * 128, 256).astype(BF16)

    cre = c_re.astype(F32)
    cim = c_im.astype(F32)
    care = jnp.stack([cre * pr[t][:, None, :] - cim * pi[t][:, None, :] for t in range(q + 1)])
    caim = jnp.stack([cre * pi[t][:, None, :] + cim * pr[t][:, None, :] for t in range(q + 1)])
    rep_cols = jnp.tile(jnp.eye(SSM_GROUP, dtype=F32), (1, 16))

    def block_diag(v, per_group):
        out = jnp.einsum("rhkc,cn->rhkn", v, rep_cols, precision=hp)
        row_g = lax.broadcasted_iota(jnp.int32, out.shape, 2) // per_group
        col_g = lax.broadcasted_iota(jnp.int32, out.shape, 3) // SSM_GROUP
        return jnp.where(row_g == col_g, out, 0.0)

    def pack_c(cv):
        cv = cv.reshape(q, 2, 16, SSM_GROUP, SSM_STATE).transpose(0, 1, 2, 4, 3)
        return block_diag(cv.reshape(q, 2, 16 * SSM_STATE, SSM_GROUP), SSM_STATE).astype(BF16)

    wcr = pack_c(care[1:])
    wci = pack_c(-caim[1:])

    kt = (jnp.einsum("tgcp,gpd->tgcd", care[:q], bbr, precision=hp)
          - jnp.einsum("tgcp,gpd->tgcd", caim[:q], bbi, precision=hp))
    kt = kt.reshape(q, 2, 16, SSM_GROUP, SSM_GROUP).transpose(0, 1, 2, 4, 3)
    tt = block_diag(kt.reshape(q, 2, 16 * SSM_GROUP, SSM_GROUP), SSM_GROUP)
    zero = jnp.zeros((2, 256, 256), F32)
    wk = jnp.stack([
        jnp.concatenate([tt[r - s] if s <= r else zero for s in range(q)], axis=1)
        for r in range(q)]).astype(BF16)

    aqr, aqi = powers(pr[q].reshape(1, N_STATE), pi[q].reshape(1, N_STATE), SUBLANES)
    rows = jnp.arange(SUBLANES)[:, None]
    tabs = []
    for d in (1, 2, 4):
        keep = (rows >= d).astype(F32)
        tabs += [keep * aqr[d], keep * aqi[d]]
    tabs += [jnp.concatenate(aqr[1:], axis=0), jnp.concatenate(aqi[1:], axis=0)]
    tab = jnp.stack(tabs)

    d4 = jnp.tile(d_skip.astype(F32).reshape(1, SSM_WIDTH), (1, q))
    return dict(wb=wb, wcr=wcr, wci=wci, wk=wk, tab=tab, d=d4)


def _mix_kernel(n_ref, y_ref, cg_ref, hist_ref, *rest, tt):
    n_gate = 2 * D_MODEL // MIX_BLOCK
    wg_refs = rest[:n_gate]
    (wglu_ref, wsp_ref, dww_ref, dwb_ref, lng_ref, lnb_ref, wcp_ref,
     merged_ref, hist_out_ref, buf, conv_scr, cb_scr, gate_scr) = rest[n_gate:]
    hist_rows = CONV_HIST * SUBLANES

    @pl.when(pl.program_id(1) == 0)
    def _():
        buf[0:hist_rows, :] = hist_ref[...]

    @pl.when(pl.program_id(1) != 0)
    def _():
        buf[0:hist_rows, :] = buf[tt * SUBLANES:tt * SUBLANES + hist_rows, :]

    buf[hist_rows:hist_rows + tt * SUBLANES, :] = cg_ref[0]
    hist_out_ref[0] = buf[tt * SUBLANES:tt * SUBLANES + hist_rows, :]

    tb = 16
    off0 = CONV_HIST - (CONV_K - 1)
    blk = MIX_BLOCK
    n = n_ref[0]

    def zero_of(v):
        u = pltpu.bitcast(v, jnp.uint32)
        return pltpu.bitcast((u >> 16) >> 16, F32)

    def plus(a, b):
        return b if a is None else (a if b is None else a + b)

    def conv_block(base, after):
        bias = dwb_ref[...] if after is None else dwb_ref[...] + after
        for sub in range(base, base + tb, SUBLANES):
            accs = [bias for _ in range(SUBLANES)]
            for k in range(CONV_K):
                wk = dww_ref[k]
                for t in range(SUBLANES):
                    r0 = (sub + t + off0 + k) * SUBLANES
                    accs[t] = accs[t] + buf[r0:r0 + SUBLANES, :] * wk
            for t in range(SUBLANES):
                r0 = (sub + t) * SUBLANES
                conv_scr[r0:r0 + SUBLANES, :] = accs[t]

        c = jnp.concatenate(
            [conv_scr[pl.ds(base * SUBLANES + s, tb, stride=SUBLANES), :] for s in range(SUBLANES)],
            axis=1)
        mu = jnp.mean(c, axis=-1, keepdims=True)
        cen = c - mu
        var = jnp.mean(cen * cen, axis=-1, keepdims=True)
        ln = cen * lax.rsqrt(var + LN_EPS) * lng_ref[...] + lnb_ref[...]
        cb_scr[base:base + tb, :] = (ln * jax.nn.sigmoid(ln)).astype(BF16)
        return zero_of(ln[tb - SUBLANES:tb, CONV_WIDTH - 128:CONV_WIDTH])

    n_conv = tt // tb
    gate_done = [None] * n_gate
    conv_done = [None] * n_gate
    for c in range(n_gate):
        cols = slice(blk * c, blk * (c + 1))
        after = conv_done[c - 2] if c >= 2 else None
        if after is None:
            lhs = n
        else:
            lead = n[0:16, 0:128] + jnp.concatenate([after, after], axis=0).astype(BF16)
            lhs = jnp.concatenate([lead, n[0:16, 128:]], axis=1)
            if tt > 16:
                lhs = jnp.concatenate([lhs, n[16:]], axis=0)
        g = _dot(lhs, wg_refs[c][...])
        gate_scr[:, cols] = jax.nn.sigmoid(g)
        gate_done[c] = zero_of(g[tt - SUBLANES:tt, blk - 128:blk])
        after = plus(gate_done[c - 1], conv_done[c - 1]) if c >= 1 else None
        done = None
        for cbi in range(c * n_conv // n_gate, (c + 1) * n_conv // n_gate):
            done = plus(done, conv_block(cbi * tb, after))
        conv_done[c] = done if done is not None or c == 0 else conv_done[c - 1]

    y = jnp.concatenate([y_ref[0, cbk] for cbk in range(SSM_WIDTH // 128)], axis=1)
    yg = (y * jax.nn.sigmoid(_dot(y.astype(BF16), wglu_ref[...]))).astype(BF16)
    cb = cb_scr[...]
    for j in range(D_MODEL // blk):
        cols = slice(blk * j, blk * (j + 1))
        gcols = slice(D_MODEL + blk * j, D_MODEL + blk * (j + 1))
        ba = _dot(yg, wsp_ref[:, cols])
        bb = _dot(cb, wcp_ref[:, cols])
        merged_ref[0, :, cols] = (gate_scr[:, cols] * ba + gate_scr[:, gcols] * bb).astype(BF16)


def _mix(n, y, cg, hist, w_in, wglu, wsp, dww, dwb, lng, lnb, wcp, tt):
    b, l, _ = n.shape
    gate_blk0 = (SSM_WIDTH + 2 * CONV_WIDTH) // MIX_BLOCK
    kern = functools.partial(_mix_kernel, tt=tt)
    tok = lambda width: pl.BlockSpec((1, tt, width), lambda bi, i: (bi, i, 0))
    return pl.pallas_call(
        kern,
        grid=(b, l // tt),
        in_specs=[
            tok(D_MODEL),
            pl.BlockSpec((1, SSM_WIDTH // 128, tt, 128), lambda bi, i: (bi, 0, i, 0)),
            pl.BlockSpec((1, tt * SUBLANES, CONV_LANES), lambda bi, i: (bi, i, 0)),
            _const_spec((CONV_HIST * SUBLANES, CONV_LANES)),
            *[pl.BlockSpec((D_MODEL, MIX_BLOCK), lambda bi, i, k=k: (0, gate_blk0 + k),
                           pipeline_mode=pl.Buffered(1))
              for k in range(2 * D_MODEL // MIX_BLOCK)],
            _const_spec((SSM_WIDTH, SSM_WIDTH)),
            _const_spec((SSM_WIDTH, D_MODEL)),
            _const_spec((CONV_K, SUBLANES, CONV_LANES)),
            _const_spec((SUBLANES, CONV_LANES)),
            _const_spec((1, CONV_WIDTH)),
            _const_spec((1, CONV_WIDTH)),
            _const_spec((CONV_WIDTH, D_MODEL)),
        ],
        out_specs=[
            tok(D_MODEL),
            pl.BlockSpec((1, CONV_HIST * SUBLANES, CONV_LANES), lambda bi, i: (bi, 0, 0)),
        ],
        out_shape=[
            jax.ShapeDtypeStruct((b, l, D_MODEL), BF16),
            jax.ShapeDtypeStruct((b, CONV_HIST * SUBLANES, CONV_LANES), F32),
        ],
        scratch_shapes=[
            pltpu.VMEM(((CONV_HIST + tt) * SUBLANES, CONV_LANES), F32),
            pltpu.VMEM((tt * SUBLANES, CONV_LANES), F32),
            pltpu.VMEM((tt, CONV_WIDTH), BF16),
            pltpu.VMEM((tt, 2 * D_MODEL), F32),
        ],
        compiler_params=_params(2, MIX_VMEM_LIMIT),
    )(n, y, cg, hist, *([w_in] * (2 * D_MODEL // MIX_BLOCK)), wglu, wsp,
      dww.reshape(CONV_K, SUBLANES, CONV_LANES), dwb.reshape(SUBLANES, CONV_LANES), lng, lnb, wcp)


def _mix_out_kernel(x_ref, m_ref, w_ref, g_ref, g2_ref, h_ref, n2_ref):
    o = _dot(m_ref[0], w_ref[...])
    h = x_ref[0] + _rms(o, g_ref[...])
    h_ref[0] = h
    n2_ref[0] = _rms(h, g2_ref[...]).astype(BF16)


def _mix_out(x, merged, w, g, g2, tt):
    b, l, _ = x.shape
    tok = pl.BlockSpec((1, tt, D_MODEL), lambda bi, i: (bi, i, 0))
    return pl.pallas_call(
        _mix_out_kernel,
        grid=(b, l // tt),
        in_specs=[tok, tok, _const_spec((D_MODEL, D_MODEL)), _const_spec((1, D_MODEL)),
                  _const_spec((1, D_MODEL))],
        out_specs=[tok, tok],
        out_shape=[jax.ShapeDtypeStruct((b, l, D_MODEL), F32),
                   jax.ShapeDtypeStruct((b, l, D_MODEL), BF16)],
        compiler_params=_params(2),
    )(x, merged, w, g, g2)


def _ffn_kernel(h_ref, n_ref, hist_ref, wg_ref, wv_ref, dw_ref, b_ref,
                wd_ref, gpost_ref, out_ref, hist_out_ref, ubuf, act_scr, carry, *, tt, nf):
    i = pl.program_id(1)
    j = pl.program_id(2)
    fb = FFN_BLOCK

    @pl.when(j == 0)
    def _():
        out_ref[0] = jnp.zeros((tt, D_MODEL), F32)

    @pl.when(i == 0)
    def _():
        carry[j] = hist_ref[j]

    ubuf[0:FFN_HIST, :] = carry[j]
    dwg, dwv = dw_ref[j], dw_ref[nf + j]
    taps = [jnp.concatenate([dwg[k:k + 1, :], dwv[k:k + 1, :]], axis=1) for k in range(FFN_K)]
    bias = jnp.concatenate([b_ref[j], b_ref[nf + j]], axis=1)

    slab = min(tt, FFN_SLAB)

    def up(r0):
        n = n_ref[0, r0:r0 + slab, :]
        ubuf[FFN_HIST + r0:FFN_HIST + r0 + slab, 0:fb] = _dot(n, wg_ref[...])
        ubuf[FFN_HIST + r0:FFN_HIST + r0 + slab, fb:2 * fb] = _dot(n, wv_ref[...])

    up(0)
    for r0 in range(0, tt, slab):
        if r0 + slab < tt:
            up(r0 + slab)
        for c0 in range(r0, r0 + slab, FFN_CHUNK):
            win = ubuf[c0:c0 + FFN_HIST + FFN_CHUNK, :]
            conv = bias + win[FFN_HIST:] * taps[FFN_K - 1]
            for k in range(FFN_K - 1):
                d = FFN_K - 1 - k
                conv = conv + pltpu.roll(win, d, 0)[FFN_HIST:] * taps[k]
            act_scr[c0:c0 + FFN_CHUNK, :] = (
                jax.nn.gelu(conv[:, 0:fb]) * conv[:, fb:2 * fb]).astype(BF16)
        out_ref[0, r0:r0 + slab, :] += _dot(act_scr[r0:r0 + slab, :], wd_ref[...])

    tail = ubuf[tt:tt + FFN_HIST, :]
    carry[j] = tail
    hist_out_ref[0, j] = tail

    @pl.when(j == nf - 1)
    def _():
        out_ref[0] = h_ref[0] + _rms(out_ref[0], gpost_ref[...])


def _ffn(h, n2, hist, wup, dww, dwb, wdown, gpost, tt):
    b, l, _ = h.shape
    fb = FFN_BLOCK
    nf = D_FF // fb
    kern = functools.partial(_ffn_kernel, tt=tt, nf=nf)
    tok = pl.BlockSpec((1, tt, D_MODEL), lambda bi, i, j: (bi, i, 0))
    return pl.pallas_call(
        kern,
        grid=(b, l // tt, nf),
        in_specs=[
            pl.BlockSpec((1, tt, D_MODEL), lambda bi, i, j: (bi, i, 0), pipeline_mode=pl.Buffered(1)),
            tok,
            _const_spec((nf, FFN_HIST, 2 * fb)),
            pl.BlockSpec((D_MODEL, fb), lambda bi, i, j: (0, j)),
            pl.BlockSpec((D_MODEL, fb), lambda bi, i, j: (0, nf + j)),
            _const_spec((2 * nf, FFN_K, fb)),
            _const_spec((2 * nf, 1, fb)),
            pl.BlockSpec((fb, D_MODEL), lambda bi, i, j: (j, 0)),
            _const_spec((1, D_MODEL)),
        ],
        out_specs=[
            tok,
            pl.BlockSpec((1, nf, FFN_HIST, 2 * fb), lambda bi, i, j: (bi, 0, 0, 0)),
        ],
        out_shape=[
            jax.ShapeDtypeStruct((b, l, D_MODEL), F32),
            jax.ShapeDtypeStruct((b, nf, FFN_HIST, 2 * fb), F32),
        ],
        scratch_shapes=[
            pltpu.VMEM((FFN_HIST + tt, 2 * fb), F32),
            pltpu.VMEM((tt, fb), BF16),
            pltpu.VMEM((nf, FFN_HIST, 2 * fb), F32),
        ],
        compiler_params=_params(3, FFN_VMEM_LIMIT),
    )(h, n2, hist, wup, wup, dww, dwb, wdown, gpost)


def _block(x, carries, p, ssm, tiles):
    b, l, _ = x.shape
    q = SSM_Q
    tt_in, m_ssm, tt_mix, tt_out, tt_ffn = tiles
    state0, conv_hist0, ffn_hist0 = carries

    n, u4, cg = _in_proj(x, p["g_mix_pre"], p["w_in"], tt_in)

    rows = l // q
    pad = (-rows) % m_ssm
    if pad:
        u4 = jnp.pad(u4, ((0, 0), (0, pad), (0, 0)))
    last_row = (rows - 1) % SUBLANES
    y, state = _ssm(u4, state0, ssm, m_ssm, last_row)
    if pad:
        y = y[:, :, :l]

    merged, conv_hist = _mix(n, y, cg, conv_hist0, p["w_in"], p["w_glu"],
                             p["w_sp"], p["conv_w"], p["conv_b"], p["ln_g"], p["ln_b"],
                             p["w_cp"], tt_mix)
    h1, n2 = _mix_out(x, merged, p["w_mo"], p["g_mix_post"], p["g_ffn_pre"], tt_out)
    out, ffn_hist = _ffn(h1, n2, ffn_hist0, p["w_up"], p["ffn_w"], p["ffn_b"],
                         p["w_down"], p["g_ffn_post"], tt_ffn)
    return out, (state[0], conv_hist[0], ffn_hist[0])


def kernel(x, meta_tokens, norm_mix_pre, w_in, lam_re, lam_im, log_step, ssm_b_re, ssm_b_im,
           ssm_c_re, ssm_c_im, ssm_d, w_ssm_glu, w_ssm_proj, conv_dw_w, conv_dw_b, conv_ln_g,
           conv_ln_b, w_conv_proj, w_mix_out, norm_mix_post, norm_ffn_pre, w_ffn_up, ffn_dw_w,
           ffn_dw_b, w_ffn_down, norm_ffn_post):
    depth = w_in.shape[0]
    nf = D_FF // FFN_BLOCK
    zero_carries = (
        jnp.zeros((SUBLANES, 2 * N_STATE), F32),
        jnp.zeros((CONV_HIST * SUBLANES, CONV_LANES), F32),
        jnp.zeros((nf, FFN_HIST, 2 * FFN_BLOCK), F32),
    )
    h_meta = meta_tokens.astype(x.dtype)[None]
    h = x
    row = lambda v: v.astype(F32).reshape(1, -1)
    blocked = lambda v: v.reshape(v.shape[0], 2 * nf, FFN_BLOCK).transpose(1, 0, 2)
    for i in range(depth):
        p = dict(
            g_mix_pre=row(norm_mix_pre[i]),
            w_in=w_in[i].astype(BF16),
            w_glu=w_ssm_glu[i].astype(BF16),
            w_sp=w_ssm_proj[i].astype(BF16),
            conv_w=conv_dw_w[i].astype(F32),
            conv_b=row(conv_dw_b[i]),
            ln_g=row(conv_ln_g[i]),
            ln_b=row(conv_ln_b[i]),
            w_cp=w_conv_proj[i].astype(BF16),
            w_mo=w_mix_out[i].astype(BF16),
            g_mix_post=row(norm_mix_post[i]),
            g_ffn_pre=row(norm_ffn_pre[i]),
            w_up=w_ffn_up[i].astype(BF16),
            ffn_w=blocked(ffn_dw_w[i].astype(F32)),
            ffn_b=blocked(row(ffn_dw_b[i])),
            w_down=w_ffn_down[i].astype(BF16),
            g_ffn_post=row(norm_ffn_post[i]),
        )
        ssm = _ssm_prep(lam_re[i], lam_im[i], log_step[i], ssm_b_re[i], ssm_b_im[i],
                        ssm_c_re[i], ssm_c_im[i], ssm_d[i])
        h_meta, carries = _block(h_meta, zero_carries, p, ssm,
                                 (N_META, SUBLANES, N_META, N_META, N_META))
        h, _ = _block(h, carries, p, ssm, (512, 512, 512, 512, 1024))
    return h
```

```python
import functools

import jax
import jax.numpy as jnp
from jax import lax
from jax.experimental import pallas as pl
from jax.experimental.pallas import tpu as pltpu

F32 = jnp.float32
BF16 = jnp.bfloat16

D_MODEL = 2048
N_META = 16
SSM_WIDTH = 512
SSM_GROUP = 16
SSM_GROUPS = 32
SSM_STATE = 64
N_STATE = SSM_GROUPS * SSM_STATE
CONV_WIDTH = 1024
CONV_K = 31
D_FF = 5632
FFN_K = 3
NORM_EPS = 1e-6
LN_EPS = 1e-5

SSM_Q = 4
CONV_HIST = 32
FFN_HIST = 8
MIX_BLOCK = 512
FFN_BLOCK = 512
FFN_SLAB = 512
FFN_CHUNK = 16
SUBLANES = 8
CONV_LANES = CONV_WIDTH // SUBLANES

VMEM_LIMIT = 56 * 1024 * 1024
MIX_VMEM_LIMIT = 60 * 1024 * 1024
FFN_VMEM_LIMIT = 62 * 1024 * 1024


def _params(n_axes, vmem_limit=VMEM_LIMIT):
    return pltpu.CompilerParams(
        dimension_semantics=("arbitrary",) * n_axes,
        vmem_limit_bytes=vmem_limit)


def _const_spec(shape):
    zeros = (0,) * len(shape)
    return pl.BlockSpec(shape, lambda *_: zeros, pipeline_mode=pl.Buffered(1))


def _rms(x, g):
    ms = jnp.mean(x * x, axis=-1, keepdims=True)
    return x * lax.rsqrt(ms + NORM_EPS) * g


def _dot(a, b):
    return jnp.dot(a, b, preferred_element_type=F32)


def _in_proj_kernel(x_ref, g_ref, w_ref, n_ref, u_ref, cg_ref, u_scr, *, tt):
    q = SSM_Q
    lanes = 128
    n = _rms(x_ref[0], g_ref[...]).astype(BF16)
    n_ref[0] = n
    u = _dot(n, w_ref[:, 0:SSM_WIDTH])
    for cb in range(SSM_WIDTH // lanes):
        u_scr[cb] = u[:, lanes * cb:lanes * (cb + 1)]
    for s in range(q):
        for cb in range(SSM_WIDTH // lanes):
            c0 = SSM_WIDTH * s + lanes * cb
            u_ref[0, :, c0:c0 + lanes] = u_scr[cb, pl.ds(s, tt // q, stride=q), :]
    blk = 512
    for j in range(CONV_WIDTH // blk):
        v0 = SSM_WIDTH + blk * j
        g0 = SSM_WIDTH + CONV_WIDTH + blk * j
        val = _dot(n, w_ref[:, v0:v0 + blk])
        gate = _dot(n, w_ref[:, g0:g0 + blk])
        cg = val * jax.nn.sigmoid(gate)
        for e in range(blk // lanes):
            s = j * (blk // lanes) + e
            cg_ref[0, pl.ds(s, tt, stride=SUBLANES), :] = cg[:, lanes * e:lanes * (e + 1)]


def _in_proj(x, g, w, tt):
    b, l, _ = x.shape
    ncol = SSM_WIDTH + 2 * CONV_WIDTH
    q = SSM_Q
    return pl.pallas_call(
        functools.partial(_in_proj_kernel, tt=tt),
        grid=(b, l // tt),
        in_specs=[
            pl.BlockSpec((1, tt, D_MODEL), lambda bi, i: (bi, i, 0)),
            _const_spec((1, D_MODEL)),
            _const_spec((D_MODEL, ncol)),
        ],
        out_specs=[
            pl.BlockSpec((1, tt, D_MODEL), lambda bi, i: (bi, i, 0)),
            pl.BlockSpec((1, tt // q, q * SSM_WIDTH), lambda bi, i: (bi, i, 0)),
            pl.BlockSpec((1, tt * SUBLANES, CONV_LANES), lambda bi, i: (bi, i, 0)),
        ],
        out_shape=[
            jax.ShapeDtypeStruct((b, l, D_MODEL), BF16),
            jax.ShapeDtypeStruct((b, l // q, q * SSM_WIDTH), F32),
            jax.ShapeDtypeStruct((b, l * SUBLANES, CONV_LANES), F32),
        ],
        scratch_shapes=[pltpu.VMEM((SSM_WIDTH // 128, tt, 128), F32)],
        compiler_params=_params(2),
    )(x, g, w)


def _ssm_kernel(u_ref, init_ref, tab_ref, wb_ref, wcr_ref, wci_ref, wk_ref, d_ref,
                y_ref, state_out_ref, x_scr, state_scr, *, m, last_row):
    q = SSM_Q
    lane_blk = 128
    tile_n = 256

    @pl.when(pl.program_id(1) == 0)
    def _():
        state_scr[...] = init_ref[...]

    u = u_ref[0]
    ub = u.astype(BF16)

    for n in range(2 * N_STATE // tile_n):
        kb = (n % (N_STATE // tile_n)) // 2
        lhs = jnp.concatenate(
            [ub[:, SSM_WIDTH * s + lane_blk * kb:SSM_WIDTH * s + lane_blk * (kb + 1)]
             for s in range(q)], axis=1)
        x_scr[:, tile_n * n:tile_n * (n + 1)] = _dot(lhs, wb_ref[n])

    w = 512
    row_id = lax.broadcasted_iota(jnp.int32, (SUBLANES, w), 0)
    for j in range(N_STATE // w):
        sr = slice(w * j, w * (j + 1))
        si = slice(N_STATE + w * j, N_STATE + w * (j + 1))
        tabs = [tab_ref[t, :, sr] for t in range(8)]

        cr, ci = state_scr[:, sr], state_scr[:, si]
        for rb in range(m // SUBLANES):
            rows = slice(rb * SUBLANES, (rb + 1) * SUBLANES)
            xr = x_scr[rows, sr]
            xi = x_scr[rows, si]
            for lvl, d in enumerate((1, 2, 4)):
                ar, ai = tabs[2 * lvl], tabs[2 * lvl + 1]
                pr = pltpu.roll(xr, d, 0)
                pi = pltpu.roll(xi, d, 0)
                xr, xi = xr + ar * pr - ai * pi, xi + ar * pi + ai * pr
            ar, ai = tabs[6], tabs[7]
            xr, xi = xr + ar * cr - ai * ci, xi + ar * ci + ai * cr
            x_scr[rows, sr] = jnp.where(row_id == 0, cr, pltpu.roll(xr, 1, 0))
            x_scr[rows, si] = jnp.where(row_id == 0, ci, pltpu.roll(xi, 1, 0))
            if rb == m // SUBLANES - 1:
                state_scr[:, sr] = jnp.broadcast_to(xr[last_row:last_row + 1], (SUBLANES, w))
                state_scr[:, si] = jnp.broadcast_to(xi[last_row:last_row + 1], (SUBLANES, w))
            cr = jnp.broadcast_to(xr[SUBLANES - 1:SUBLANES], (SUBLANES, w))
            ci = jnp.broadcast_to(xi[SUBLANES - 1:SUBLANES], (SUBLANES, w))

    state_out_ref[0] = state_scr[...]

    half = 256
    st_per_half = half // SSM_GROUP * SSM_STATE
    for h in range(SSM_WIDTH // half):
        er = x_scr[:, st_per_half * h:st_per_half * (h + 1)].astype(BF16)
        ei = x_scr[:, N_STATE + st_per_half * h:N_STATE + st_per_half * (h + 1)].astype(BF16)
        for r in range(q):
            lanes = slice(SSM_WIDTH * r + half * h, SSM_WIDTH * r + half * (h + 1))
            lhs = jnp.concatenate(
                [ub[:, SSM_WIDTH * s + half * h:SSM_WIDTH * s + half * (h + 1)]
                 for s in range(r + 1)], axis=1)
            acc = _dot(er, wcr_ref[r, h]) + _dot(ei, wci_ref[r, h])
            acc = acc + _dot(lhs, wk_ref[r, h, 0:half * (r + 1), :])
            y = jax.nn.gelu(acc + d_ref[:, lanes] * u[:, lanes])
            for e in range(half // lane_blk):
                y_ref[0, (half // lane_blk) * h + e, pl.ds(r, m, stride=q), :] = (
                    y[:, lane_blk * e:lane_blk * (e + 1)])


def _ssm(u4, init_state, prep, m, last_row):
    b, rows, width = u4.shape
    q = SSM_Q
    kern = functools.partial(_ssm_kernel, m=m, last_row=last_row)
    return pl.pallas_call(
        kern,
        grid=(b, rows // m),
        in_specs=[
            pl.BlockSpec((1, m, width), lambda bi, i: (bi, i, 0)),
            _const_spec((SUBLANES, 2 * N_STATE)),
            _const_spec((8, SUBLANES, N_STATE)),
            _const_spec((16, q * 128, 256)),
            _const_spec((q, 2, 1024, 256)),
            _const_spec((q, 2, 1024, 256)),
            _const_spec((q, 2, q * 256, 256)),
            _const_spec((1, width)),
        ],
        out_specs=[
            pl.BlockSpec((1, SSM_WIDTH // 128, q * m, 128), lambda bi, i: (bi, 0, i, 0)),
            pl.BlockSpec((1, SUBLANES, 2 * N_STATE), lambda bi, i: (bi, 0, 0)),
        ],
        out_shape=[
            jax.ShapeDtypeStruct((b, SSM_WIDTH // 128, q * rows, 128), F32),
            jax.ShapeDtypeStruct((b, SUBLANES, 2 * N_STATE), F32),
        ],
        scratch_shapes=[
            pltpu.VMEM((m, 2 * N_STATE), F32),
            pltpu.VMEM((SUBLANES, 2 * N_STATE), F32),
        ],
        compiler_params=_params(2),
    )(u4, init_state, prep["tab"], prep["wb"], prep["wcr"], prep["wci"], prep["wk"], prep["d"])


def _ssm_prep(lam_re, lam_im, log_step, b_re, b_im, c_re, c_im, d_skip):
    q = SSM_Q
    hp = lax.Precision.HIGHEST
    lr = lam_re.astype(F32)
    li = lam_im.astype(F32)
    step = jnp.exp(log_step.astype(F32))[:, None]
    mag = jnp.exp(lr * step)
    ar = mag * jnp.cos(li * step)
    ai = mag * jnp.sin(li * step)
    den = lr * lr + li * li
    cr = ((ar - 1.0) * lr + ai * li) / den
    ci = (ai * lr - (ar - 1.0) * li) / den
    br_ = b_re.astype(F32)
    bi_ = b_im.astype(F32)
    bbr = cr[..., None] * br_ - ci[..., None] * bi_
    bbi = cr[..., None] * bi_ + ci[..., None] * br_

    def powers(xr, xi, n):
        outr, outi = [jnp.ones_like(xr)], [jnp.zeros_like(xi)]
        for _ in range(n):
            pr, pi = outr[-1], outi[-1]
            outr.append(pr * xr - pi * xi)
            outi.append(pr * xi + pi * xr)
        return outr, outi

    pr, pi = powers(ar, ai, q)

    vre = jnp.stack([pr[q - 1 - s][..., None] * bbr - pi[q - 1 - s][..., None] * bbi for s in range(q)])
    vim = jnp.stack([pr[q - 1 - s][..., None] * bbi + pi[q - 1 - s][..., None] * bbr for s in range(q)])
    val = jnp.stack([vre, vim])
    val = val.reshape(2, q, 8, 4, SSM_STATE, SSM_GROUP).transpose(0, 2, 1, 5, 3, 4)
    val = val.reshape(2, 8, q * SSM_GROUP, 256)
    rep_rows = jnp.kron(jnp.eye(q, dtype=F32), jnp.tile(jnp.eye(SSM_GROUP, dtype=F32), (8, 1)))
    wb = jnp.einsum("Rk,anke->anRe", rep_rows, val, precision=hp)
    wb_shape = (8, q * 128, 256)
    row_gl = (lax.broadcasted_iota(jnp.int32, wb_shape, 1) // SSM_GROUP) % 8
    col_gq = lax.broadcasted_iota(jnp.int32, wb_shape, 2) // SSM_STATE
    tile_nb = lax.broadcasted_iota(jnp.int32, wb_shape, 0)
    wb = jnp.where(row_gl == 4 * (tile_nb % 2) + col_gq, wb, 0.0)
    wb = wb.reshape(16, q * 128, 256).astype(BF16)

    cre = c_re.astype(F32)
    cim = c_im.astype(F32)
    care = jnp.stack([cre * pr[t][:, None, :] - cim * pi[t][:, None, :] for t in range(q + 1)])
    caim = jnp.stack([cre * pi[t][:, None, :] + cim * pr[t][:, None, :] for t in range(q + 1)])
    rep_cols = jnp.tile(jnp.eye(SSM_GROUP, dtype=F32), (1, 16))

    def block_diag(v, per_group):
        out = jnp.einsum("rhkc,cn->rhkn", v, rep_cols, precision=hp)
        row_g = lax.broadcasted_iota(jnp.int32, out.shape, 2) // per_group
        col_g = lax.broadcasted_iota(jnp.int32, out.shape, 3) // SSM_GROUP
        return jnp.where(row_g == col_g, out, 0.0)

    def pack_c(cv):
        cv = cv.reshape(q, 2, 16, SSM_GROUP, SSM_STATE).transpose(0, 1, 2, 4, 3)
        return block_diag(cv.reshape(q, 2, 16 * SSM_STATE, SSM_GROUP), SSM_STATE).astype(BF16)

    wcr = pack_c(care[1:])
    wci = pack_c(-caim[1:])

    kt = (jnp.einsum("tgcp,gpd->tgcd", care[:q], bbr, precision=hp)
          - jnp.einsum("tgcp,gpd->tgcd", caim[:q], bbi, precision=hp))
    kt = kt.reshape(q, 2, 16, SSM_GROUP, SSM_GROUP).transpose(0, 1, 2, 4, 3)
    tt = block_diag(kt.reshape(q, 2, 16 * SSM_GROUP, SSM_GROUP), SSM_GROUP)
    zero = jnp.zeros((2, 256, 256), F32)
    wk = jnp.stack([
        jnp.concatenate([tt[r - s] if s <= r else zero for s in range(q)], axis=1)
        for r in range(q)]).astype(BF16)

    aqr, aqi = powers(pr[q].reshape(1, N_STATE), pi[q].reshape(1, N_STATE), SUBLANES)
    rows = jnp.arange(SUBLANES)[:, None]
    tabs = []
    for d in (1, 2, 4):
        keep = (rows >= d).astype(F32)
        tabs += [keep * aqr[d], keep * aqi[d]]
    tabs += [jnp.concatenate(aqr[1:], axis=0), jnp.concatenate(aqi[1:], axis=0)]
    tab = jnp.stack(tabs)

    d4 = jnp.tile(d_skip.astype(F32).reshape(1, SSM_WIDTH), (1, q))
    return dict(wb=wb, wcr=wcr, wci=wci, wk=wk, tab=tab, d=d4)


def _mix_kernel(n_ref, y_ref, cg_ref, hist_ref, *rest, tt):
    n_gate = 2 * D_MODEL // MIX_BLOCK
    wg_refs = rest[:n_gate]
    (wglu_ref, wsp_ref, dww_ref, dwb_ref, lng_ref, lnb_ref, wcp_ref,
     merged_ref, hist_out_ref, buf, conv_scr, cb_scr, gate_scr) = rest[n_gate:]
    hist_rows = CONV_HIST * SUBLANES

    @pl.when(pl.program_id(1) == 0)
    def _():
        buf[0:hist_rows, :] = hist_ref[...]

    @pl.when(pl.program_id(1) != 0)
    def _():
        buf[0:hist_rows, :] = buf[tt * SUBLANES:tt * SUBLANES + hist_rows, :]

    buf[hist_rows:hist_rows + tt * SUBLANES, :] = cg_ref[0]
    hist_out_ref[0] = buf[tt * SUBLANES:tt * SUBLANES + hist_rows, :]

    tb = 16
    off0 = CONV_HIST - (CONV_K - 1)
    blk = MIX_BLOCK
    n = n_ref[0]

    def zero_of(v):
        u = pltpu.bitcast(v, jnp.uint32)
        return pltpu.bitcast((u >> 16) >> 16, F32)

    def plus(a, b):
        return b if a is None else (a if b is None else a + b)

    def conv_block(base, after):
        bias = dwb_ref[...] if after is None else dwb_ref[...] + after
        for sub in range(base, base + tb, SUBLANES):
            accs = [bias for _ in range(SUBLANES)]
            for k in range(CONV_K):
                wk = dww_ref[k]
                for t in range(SUBLANES):
                    r0 = (sub + t + off0 + k) * SUBLANES
                    accs[t] = accs[t] + buf[r0:r0 + SUBLANES, :] * wk
            for t in range(SUBLANES):
                r0 = (sub + t) * SUBLANES
                conv_scr[r0:r0 + SUBLANES, :] = accs[t]

        c = jnp.concatenate(
            [conv_scr[pl.ds(base * SUBLANES + s, tb, stride=SUBLANES), :] for s in range(SUBLANES)],
            axis=1)
        mu = jnp.mean(c, axis=-1, keepdims=True)
        cen = c - mu
        var = jnp.mean(cen * cen, axis=-1, keepdims=True)
        ln = cen * lax.rsqrt(var + LN_EPS) * lng_ref[...] + lnb_ref[...]
        cb_scr[base:base + tb, :] = (ln * jax.nn.sigmoid(ln)).astype(BF16)
        return zero_of(ln[tb - SUBLANES:tb, CONV_WIDTH - 128:CONV_WIDTH])

    n_conv = tt // tb
    gate_done = [None] * n_gate
    conv_done = [None] * n_gate
    for c in range(n_gate):
        cols = slice(blk * c, blk * (c + 1))
        after = conv_done[c - 2] if c >= 2 else None
        if after is None:
            lhs = n
        else:
            lead = n[0:16, 0:128] + jnp.concatenate([after, after], axis=0).astype(BF16)
            lhs = jnp.concatenate([lead, n[0:16, 128:]], axis=1)
            if tt > 16:
                lhs = jnp.concatenate([lhs, n[16:]], axis=0)
        g = _dot(lhs, wg_refs[c][...])
        gate_scr[:, cols] = jax.nn.sigmoid(g)
        gate_done[c] = zero_of(g[tt - SUBLANES:tt, blk - 128:blk])
        after = plus(gate_done[c - 1], conv_done[c - 1]) if c >= 1 else None
        done = None
        for cbi in range(c * n_conv // n_gate, (c + 1) * n_conv // n_gate):
            done = plus(done, conv_block(cbi * tb, after))
        conv_done[c] = done if done is not None or c == 0 else conv_done[c - 1]

    y = jnp.concatenate([y_ref[0, cbk] for cbk in range(SSM_WIDTH // 128)], axis=1)
    yg = (y * jax.nn.sigmoid(_dot(y.astype(BF16), wglu_ref[...]))).astype(BF16)
    cb = cb_scr[...]
    for j in range(D_MODEL // blk):
        cols = slice(blk * j, blk * (j + 1))
        gcols = slice(D_MODEL + blk * j, D_MODEL + blk * (j + 1))
        ba = _dot(yg, wsp_ref[:, cols])
        bb = _dot(cb, wcp_ref[:, cols])
        merged_ref[0, :, cols] = (gate_scr[:, cols] * ba + gate_scr[:, gcols] * bb).astype(BF16)


def _mix(n, y, cg, hist, w_in, wglu, wsp, dww, dwb, lng, lnb, wcp, tt):
    b, l, _ = n.shape
    gate_blk0 = (SSM_WIDTH + 2 * CONV_WIDTH) // MIX_BLOCK
    kern = functools.partial(_mix_kernel, tt=tt)
    tok = lambda width: pl.BlockSpec((1, tt, width), lambda bi, i: (bi, i, 0))
    return pl.pallas_call(
        kern,
        grid=(b, l // tt),
        in_specs=[
            tok(D_MODEL),
            pl.BlockSpec((1, SSM_WIDTH // 128, tt, 128), lambda bi, i: (bi, 0, i, 0)),
            pl.BlockSpec((1, tt * SUBLANES, CONV_LANES), lambda bi, i: (bi, i, 0)),
            _const_spec((CONV_HIST * SUBLANES, CONV_LANES)),
            *[pl.BlockSpec((D_MODEL, MIX_BLOCK), lambda bi, i, k=k: (0, gate_blk0 + k),
                           pipeline_mode=pl.Buffered(1))
              for k in range(2 * D_MODEL // MIX_BLOCK)],
            _const_spec((SSM_WIDTH, SSM_WIDTH)),
            _const_spec((SSM_WIDTH, D_MODEL)),
            _const_spec((CONV_K, SUBLANES, CONV_LANES)),
            _const_spec((SUBLANES, CONV_LANES)),
            _const_spec((1, CONV_WIDTH)),
            _const_spec((1, CONV_WIDTH)),
            _const_spec((CONV_WIDTH, D_MODEL)),
        ],
        out_specs=[
            tok(D_MODEL),
            pl.BlockSpec((1, CONV_HIST * SUBLANES, CONV_LANES), lambda bi, i: (bi, 0, 0)),
        ],
        out_shape=[
            jax.ShapeDtypeStruct((b, l, D_MODEL), BF16),
            jax.ShapeDtypeStruct((b, CONV_HIST * SUBLANES, CONV_LANES), F32),
        ],
        scratch_shapes=[
            pltpu.VMEM(((CONV_HIST + tt) * SUBLANES, CONV_LANES), F32),
            pltpu.VMEM((tt * SUBLANES, CONV_LANES), F32),
            pltpu.VMEM((tt, CONV_WIDTH), BF16),
            pltpu.VMEM((tt, 2 * D_MODEL), F32),
        ],
        compiler_params=_params(2, MIX_VMEM_LIMIT),
    )(n, y, cg, hist, *([w_in] * (2 * D_MODEL // MIX_BLOCK)), wglu, wsp,
      dww.reshape(CONV_K, SUBLANES, CONV_LANES), dwb.reshape(SUBLANES, CONV_LANES), lng, lnb, wcp)


def _mix_out_kernel(x_ref, m_ref, w_ref, g_ref, g2_ref, h_ref, n2_ref):
    o = _dot(m_ref[0], w_ref[...])
    h = x_ref[0] + _rms(o, g_ref[...])
    h_ref[0] = h
    n2_ref[0] = _rms(h, g2_ref[...]).astype(BF16)


def _mix_out(x, merged, w, g, g2, tt):
    b, l, _ = x.shape
    tok = pl.BlockSpec((1, tt, D_MODEL), lambda bi, i: (bi, i, 0))
    return pl.pallas_call(
        _mix_out_kernel,
        grid=(b, l // tt),
        in_specs=[tok, tok, _const_spec((D_MODEL, D_MODEL)), _const_spec((1, D_MODEL)),
                  _const_spec((1, D_MODEL))],
        out_specs=[tok, tok],
        out_shape=[jax.ShapeDtypeStruct((b, l, D_MODEL), F32),
                   jax.ShapeDtypeStruct((b, l, D_MODEL), BF16)],
        compiler_params=_params(2),
    )(x, merged, w, g, g2)


def _ffn_kernel(h_hbm, n_ref, hist_ref, wg_ref, wv_ref, dw_ref, b_ref,
                wd_ref, gpost_ref, out_ref, hist_out_ref, ubuf, act_scr, carry, h_buf, h_sem,
                *, tt, nf):
    bi = pl.program_id(0)
    i = pl.program_id(1)
    j = pl.program_id(2)
    fb = FFN_BLOCK

    def residual_copy():
        return pltpu.make_async_copy(h_hbm.at[bi, pl.ds(i * tt, tt), :], h_buf, h_sem)

    @pl.when(j == 0)
    def _():
        residual_copy().start()
        out_ref[0] = jnp.zeros((tt, D_MODEL), F32)

    @pl.when(i == 0)
    def _():
        carry[j] = hist_ref[j]

    ubuf[0:FFN_HIST, :] = carry[j]
    dwg, dwv = dw_ref[j], dw_ref[nf + j]
    taps = [jnp.concatenate([dwg[k:k + 1, :], dwv[k:k + 1, :]], axis=1) for k in range(FFN_K)]
    bias = jnp.concatenate([b_ref[j], b_ref[nf + j]], axis=1)

    slab = min(tt, FFN_SLAB)

    def up(r0):
        n = n_ref[0, r0:r0 + slab, :]
        ubuf[FFN_HIST + r0:FFN_HIST + r0 + slab, 0:fb] = _dot(n, wg_ref[...])
        ubuf[FFN_HIST + r0:FFN_HIST + r0 + slab, fb:2 * fb] = _dot(n, wv_ref[...])

    up(0)
    for r0 in range(0, tt, slab):
        if r0 + slab < tt:
            up(r0 + slab)
        for c0 in range(r0, r0 + slab, FFN_CHUNK):
            win = ubuf[c0:c0 + FFN_HIST + FFN_CHUNK, :]
            conv = bias + win[FFN_HIST:] * taps[FFN_K - 1]
            for k in range(FFN_K - 1):
                d = FFN_K - 1 - k
                conv = conv + pltpu.roll(win, d, 0)[FFN_HIST:] * taps[k]
            act_scr[c0:c0 + FFN_CHUNK, :] = (
                jax.nn.gelu(conv[:, 0:fb]) * conv[:, fb:2 * fb]).astype(BF16)
        out_ref[0, r0:r0 + slab, :] += _dot(act_scr[r0:r0 + slab, :], wd_ref[...])

    tail = ubuf[tt:tt + FFN_HIST, :]
    carry[j] = tail
    hist_out_ref[0, j] = tail

    @pl.when(j == nf - 1)
    def _():
        residual_copy().wait()
        out_ref[0] = h_buf[...] + _rms(out_ref[0], gpost_ref[...])


def _ffn(h, n2, hist, wup, dww, dwb, wdown, gpost, tt):
    b, l, _ = h.shape
    fb = FFN_BLOCK
    nf = D_FF // fb
    kern = functools.partial(_ffn_kernel, tt=tt, nf=nf)
    tok = pl.BlockSpec((1, tt, D_MODEL), lambda bi, i, j: (bi, i, 0))
    return pl.pallas_call(
        kern,
        grid=(b, l // tt, nf),
        in_specs=[
            pl.BlockSpec(memory_space=pl.ANY),
            tok,
            _const_spec((nf, FFN_HIST, 2 * fb)),
            pl.BlockSpec((D_MODEL, fb), lambda bi, i, j: (0, j)),
            pl.BlockSpec((D_MODEL, fb), lambda bi, i, j: (0, nf + j)),
            _const_spec((2 * nf, FFN_K, fb)),
            _const_spec((2 * nf, 1, fb)),
            pl.BlockSpec((fb, D_MODEL), lambda bi, i, j: (j, 0)),
            _const_spec((1, D_MODEL)),
        ],
        out_specs=[
            tok,
            pl.BlockSpec((1, nf, FFN_HIST, 2 * fb), lambda bi, i, j: (bi, 0, 0, 0)),
        ],
        out_shape=[
            jax.ShapeDtypeStruct((b, l, D_MODEL), F32),
            jax.ShapeDtypeStruct((b, nf, FFN_HIST, 2 * fb), F32),
        ],
        scratch_shapes=[
            pltpu.VMEM((FFN_HIST + tt, 2 * fb), F32),
            pltpu.VMEM((tt, fb), BF16),
            pltpu.VMEM((nf, FFN_HIST, 2 * fb), F32),
            pltpu.VMEM((tt, D_MODEL), F32),
            pltpu.SemaphoreType.DMA(()),
        ],
        compiler_params=_params(3, FFN_VMEM_LIMIT),
    )(h, n2, hist, wup, wup, dww, dwb, wdown, gpost)


def _block(x, carries, p, ssm, tiles):
    b, l, _ = x.shape
    q = SSM_Q
    tt_in, m_ssm, tt_mix, tt_out, tt_ffn = tiles
    state0, conv_hist0, ffn_hist0 = carries

    n, u4, cg = _in_proj(x, p["g_mix_pre"], p["w_in"], tt_in)

    rows = l // q
    pad = (-rows) % m_ssm
    if pad:
        u4 = jnp.pad(u4, ((0, 0), (0, pad), (0, 0)))
    last_row = (rows - 1) % SUBLANES
    y, state = _ssm(u4, state0, ssm, m_ssm, last_row)
    if pad:
        y = y[:, :, :l]

    merged, conv_hist = _mix(n, y, cg, conv_hist0, p["w_in"], p["w_glu"],
                             p["w_sp"], p["conv_w"], p["conv_b"], p["ln_g"], p["ln_b"],
                             p["w_cp"], tt_mix)
    h1, n2 = _mix_out(x, merged, p["w_mo"], p["g_mix_post"], p["g_ffn_pre"], tt_out)
    out, ffn_hist = _ffn(h1, n2, ffn_hist0, p["w_up"], p["ffn_w"], p["ffn_b"],
                         p["w_down"], p["g_ffn_post"], tt_ffn)
    return out, (state[0], conv_hist[0], ffn_hist[0])


def kernel(x, meta_tokens, norm_mix_pre, w_in, lam_re, lam_im, log_step, ssm_b_re, ssm_b_im,
           ssm_c_re, ssm_c_im, ssm_d, w_ssm_glu, w_ssm_proj, conv_dw_w, conv_dw_b, conv_ln_g,
           conv_ln_b, w_conv_proj, w_mix_out, norm_mix_post, norm_ffn_pre, w_ffn_up, ffn_dw_w,
           ffn_dw_b, w_ffn_down, norm_ffn_post):
    depth = w_in.shape[0]
    nf = D_FF // FFN_BLOCK
    zero_carries = (
        jnp.zeros((SUBLANES, 2 * N_STATE), F32),
        jnp.zeros((CONV_HIST * SUBLANES, CONV_LANES), F32),
        jnp.zeros((nf, FFN_HIST, 2 * FFN_BLOCK), F32),
    )
    h_meta = meta_tokens.astype(x.dtype)[None]
    h = x
    row = lambda v: v.astype(F32).reshape(1, -1)
    blocked = lambda v: v.reshape(v.shape[0], 2 * nf, FFN_BLOCK).transpose(1, 0, 2)
    for i in range(depth):
        p = dict(
            g_mix_pre=row(norm_mix_pre[i]),
            w_in=w_in[i].astype(BF16),
            w_glu=w_ssm_glu[i].astype(BF16),
            w_sp=w_ssm_proj[i].astype(BF16),
            conv_w=conv_dw_w[i].astype(F32),
            conv_b=row(conv_dw_b[i]),
            ln_g=row(conv_ln_g[i]),
            ln_b=row(conv_ln_b[i]),
            w_cp=w_conv_proj[i].astype(BF16),
            w_mo=w_mix_out[i].astype(BF16),
            g_mix_post=row(norm_mix_post[i]),
            g_ffn_pre=row(norm_ffn_pre[i]),
            w_up=w_ffn_up[i].astype(BF16),
            ffn_w=blocked(ffn_dw_w[i].astype(F32)),
            ffn_b=blocked(row(ffn_dw_b[i])),
            w_down=w_ffn_down[i].astype(BF16),
            g_ffn_post=row(norm_ffn_post[i]),
        )
        ssm = _ssm_prep(lam_re[i], lam_im[i], log_step[i], ssm_b_re[i], ssm_b_im[i],
                        ssm_c_re[i], ssm_c_im[i], ssm_d[i])
        h_meta, carries = _block(h_meta, zero_carries, p, ssm,
                                 (N_META, SUBLANES, N_META, N_META, N_META))
        h, _ = _block(h, carries, p, ssm, (512, 512, 512, 512, 1024))
    return h
```

```python
import functools

import jax
import jax.numpy as jnp
from jax import lax
from jax.experimental import pallas as pl
from jax.experimental.pallas import tpu as pltpu

F32 = jnp.float32
BF16 = jnp.bfloat16

D_MODEL = 2048
N_META = 16
SSM_WIDTH = 512
SSM_GROUP = 16
SSM_GROUPS = 32
SSM_STATE = 64
N_STATE = SSM_GROUPS * SSM_STATE
CONV_WIDTH = 1024
CONV_K = 31
D_FF = 5632
FFN_K = 3
NORM_EPS = 1e-6
LN_EPS = 1e-5

SSM_Q = 4
CONV_HIST = 32
FFN_HIST = 8
MIX_BLOCK = 512
GATE_BLOCK = 256
FFN_BLOCK = 512
FFN_SLAB = 512
FFN_CHUNK = 16
SUBLANES = 8
CONV_LANES = CONV_WIDTH // SUBLANES

VMEM_LIMIT = 56 * 1024 * 1024
MIX_VMEM_LIMIT = 60 * 1024 * 1024
FFN_VMEM_LIMIT = 62 * 1024 * 1024


def _params(n_axes, vmem_limit=VMEM_LIMIT):
    return pltpu.CompilerParams(
        dimension_semantics=("arbitrary",) * n_axes,
        vmem_limit_bytes=vmem_limit)


def _const_spec(shape):
    zeros = (0,) * len(shape)
    return pl.BlockSpec(shape, lambda *_: zeros, pipeline_mode=pl.Buffered(1))


def _rms(x, g):
    ms = jnp.mean(x * x, axis=-1, keepdims=True)
    return x * lax.rsqrt(ms + NORM_EPS) * g


def _dot(a, b):
    return jnp.dot(a, b, preferred_element_type=F32)


def _in_proj_kernel(x_ref, g_ref, w_ref, n_ref, u_ref, cg_ref, u_scr, *, tt):
    q = SSM_Q
    lanes = 128
    n = _rms(x_ref[0], g_ref[...]).astype(BF16)
    n_ref[0] = n
    u = _dot(n, w_ref[:, 0:SSM_WIDTH])
    for cb in range(SSM_WIDTH // lanes):
        u_scr[cb] = u[:, lanes * cb:lanes * (cb + 1)]
    for s in range(q):
        for cb in range(SSM_WIDTH // lanes):
            c0 = SSM_WIDTH * s + lanes * cb
            u_ref[0, :, c0:c0 + lanes] = u_scr[cb, pl.ds(s, tt // q, stride=q), :]
    blk = 512
    for j in range(CONV_WIDTH // blk):
        v0 = SSM_WIDTH + blk * j
        g0 = SSM_WIDTH + CONV_WIDTH + blk * j
        val = _dot(n, w_ref[:, v0:v0 + blk])
        gate = _dot(n, w_ref[:, g0:g0 + blk])
        cg = val * jax.nn.sigmoid(gate)
        for e in range(blk // lanes):
            s = j * (blk // lanes) + e
            cg_ref[0, pl.ds(s, tt, stride=SUBLANES), :] = cg[:, lanes * e:lanes * (e + 1)]


def _in_proj(x, g, w, tt):
    b, l, _ = x.shape
    ncol = SSM_WIDTH + 2 * CONV_WIDTH
    q = SSM_Q
    return pl.pallas_call(
        functools.partial(_in_proj_kernel, tt=tt),
        grid=(b, l // tt),
        in_specs=[
            pl.BlockSpec((1, tt, D_MODEL), lambda bi, i: (bi, i, 0)),
            _const_spec((1, D_MODEL)),
            _const_spec((D_MODEL, ncol)),
        ],
        out_specs=[
            pl.BlockSpec((1, tt, D_MODEL), lambda bi, i: (bi, i, 0)),
            pl.BlockSpec((1, tt // q, q * SSM_WIDTH), lambda bi, i: (bi, i, 0)),
            pl.BlockSpec((1, tt * SUBLANES, CONV_LANES), lambda bi, i: (bi, i, 0)),
        ],
        out_shape=[
            jax.ShapeDtypeStruct((b, l, D_MODEL), BF16),
            jax.ShapeDtypeStruct((b, l // q, q * SSM_WIDTH), F32),
            jax.ShapeDtypeStruct((b, l * SUBLANES, CONV_LANES), F32),
        ],
        scratch_shapes=[pltpu.VMEM((SSM_WIDTH // 128, tt, 128), F32)],
        compiler_params=_params(2),
    )(x, g, w)


def _ssm_kernel(u_ref, init_ref, tab_ref, wb_ref, wcr_ref, wci_ref, wk_ref, d_ref,
                y_ref, state_out_ref, x_scr, state_scr, *, m, last_row):
    q = SSM_Q
    lane_blk = 128
    tile_n = 256

    @pl.when(pl.program_id(1) == 0)
    def _():
        state_scr[...] = init_ref[...]

    u = u_ref[0]
    ub = u.astype(BF16)

    for n in range(2 * N_STATE // tile_n):
        kb = (n % (N_STATE // tile_n)) // 2
        lhs = jnp.concatenate(
            [ub[:, SSM_WIDTH * s + lane_blk * kb:SSM_WIDTH * s + lane_blk * (kb + 1)]
             for s in range(q)], axis=1)
        x_scr[:, tile_n * n:tile_n * (n + 1)] = _dot(lhs, wb_ref[n])

    w = 512
    row_id = lax.broadcasted_iota(jnp.int32, (SUBLANES, w), 0)
    for j in range(N_STATE // w):
        sr = slice(w * j, w * (j + 1))
        si = slice(N_STATE + w * j, N_STATE + w * (j + 1))
        tabs = [tab_ref[t, :, sr] for t in range(8)]

        cr, ci = state_scr[:, sr], state_scr[:, si]
        for rb in range(m // SUBLANES):
            rows = slice(rb * SUBLANES, (rb + 1) * SUBLANES)
            xr = x_scr[rows, sr]
            xi = x_scr[rows, si]
            for lvl, d in enumerate((1, 2, 4)):
                ar, ai = tabs[2 * lvl], tabs[2 * lvl + 1]
                pr = pltpu.roll(xr, d, 0)
                pi = pltpu.roll(xi, d, 0)
                xr, xi = xr + ar * pr - ai * pi, xi + ar * pi + ai * pr
            ar, ai = tabs[6], tabs[7]
            xr, xi = xr + ar * cr - ai * ci, xi + ar * ci + ai * cr
            x_scr[rows, sr] = jnp.where(row_id == 0, cr, pltpu.roll(xr, 1, 0))
            x_scr[rows, si] = jnp.where(row_id == 0, ci, pltpu.roll(xi, 1, 0))
            if rb == m // SUBLANES - 1:
                state_scr[:, sr] = jnp.broadcast_to(xr[last_row:last_row + 1], (SUBLANES, w))
                state_scr[:, si] = jnp.broadcast_to(xi[last_row:last_row + 1], (SUBLANES, w))
            cr = jnp.broadcast_to(xr[SUBLANES - 1:SUBLANES], (SUBLANES, w))
            ci = jnp.broadcast_to(xi[SUBLANES - 1:SUBLANES], (SUBLANES, w))

    state_out_ref[0] = state_scr[...]

    half = 256
    st_per_half = half // SSM_GROUP * SSM_STATE
    for h in range(SSM_WIDTH // half):
        er = x_scr[:, st_per_half * h:st_per_half * (h + 1)].astype(BF16)
        ei = x_scr[:, N_STATE + st_per_half * h:N_STATE + st_per_half * (h + 1)].astype(BF16)
        for r in range(q):
            lanes = slice(SSM_WIDTH * r + half * h, SSM_WIDTH * r + half * (h + 1))
            lhs = jnp.concatenate(
                [ub[:, SSM_WIDTH * s + half * h:SSM_WIDTH * s + half * (h + 1)]
                 for s in range(r + 1)], axis=1)
            acc = _dot(er, wcr_ref[r, h]) + _dot(ei, wci_ref[r, h])
            acc = acc + _dot(lhs, wk_ref[r, h, 0:half * (r + 1), :])
            y = jax.nn.gelu(acc + d_ref[:, lanes] * u[:, lanes])
            for e in range(half // lane_blk):
                y_ref[0, (half // lane_blk) * h + e, pl.ds(r, m, stride=q), :] = (
                    y[:, lane_blk * e:lane_blk * (e + 1)])


def _ssm(u4, init_state, prep, m, last_row):
    b, rows, width = u4.shape
    q = SSM_Q
    kern = functools.partial(_ssm_kernel, m=m, last_row=last_row)
    return pl.pallas_call(
        kern,
        grid=(b, rows // m),
        in_specs=[
            pl.BlockSpec((1, m, width), lambda bi, i: (bi, i, 0)),
            _const_spec((SUBLANES, 2 * N_STATE)),
            _const_spec((8, SUBLANES, N_STATE)),
            _const_spec((16, q * 128, 256)),
            _const_spec((q, 2, 1024, 256)),
            _const_spec((q, 2, 1024, 256)),
            _const_spec((q, 2, q * 256, 256)),
            _const_spec((1, width)),
        ],
        out_specs=[
            pl.BlockSpec((1, SSM_WIDTH // 128, q * m, 128), lambda bi, i: (bi, 0, i, 0)),
            pl.BlockSpec((1, SUBLANES, 2 * N_STATE), lambda bi, i: (bi, 0, 0)),
        ],
        out_shape=[
            jax.ShapeDtypeStruct((b, SSM_WIDTH // 128, q * rows, 128), F32),
            jax.ShapeDtypeStruct((b, SUBLANES, 2 * N_STATE), F32),
        ],
        scratch_shapes=[
            pltpu.VMEM((m, 2 * N_STATE), F32),
            pltpu.VMEM((SUBLANES, 2 * N_STATE), F32),
        ],
        compiler_params=_params(2),
    )(u4, init_state, prep["tab"], prep["wb"], prep["wcr"], prep["wci"], prep["wk"], prep["d"])


def _ssm_prep(lam_re, lam_im, log_step, b_re, b_im, c_re, c_im, d_skip):
    q = SSM_Q
    hp = lax.Precision.HIGHEST
    lr = lam_re.astype(F32)
    li = lam_im.astype(F32)
    step = jnp.exp(log_step.astype(F32))[:, None]
    mag = jnp.exp(lr * step)
    ar = mag * jnp.cos(li * step)
    ai = mag * jnp.sin(li * step)
    den = lr * lr + li * li
    cr = ((ar - 1.0) * lr + ai * li) / den
    ci = (ai * lr - (ar - 1.0) * li) / den
    br_ = b_re.astype(F32)
    bi_ = b_im.astype(F32)
    bbr = cr[..., None] * br_ - ci[..., None] * bi_
    bbi = cr[..., None] * bi_ + ci[..., None] * br_

    def powers(xr, xi, n):
        outr, outi = [jnp.ones_like(xr)], [jnp.zeros_like(xi)]
        for _ in range(n):
            pr, pi = outr[-1], outi[-1]
            outr.append(pr * xr - pi * xi)
            outi.append(pr * xi + pi * xr)
        return outr, outi

    pr, pi = powers(ar, ai, q)

    vre = jnp.stack([pr[q - 1 - s][..., None] * bbr - pi[q - 1 - s][..., None] * bbi for s in range(q)])
    vim = jnp.stack([pr[q - 1 - s][..., None] * bbi + pi[q - 1 - s][..., None] * bbr for s in range(q)])
    val = jnp.stack([vre, vim])
    val = val.reshape(2, q, 8, 4, SSM_STATE, SSM_GROUP).transpose(0, 2, 1, 5, 3, 4)
    val = val.reshape(2, 8, q * SSM_GROUP, 256)
    rep_rows = jnp.kron(jnp.eye(q, dtype=F32), jnp.tile(jnp.eye(SSM_GROUP, dtype=F32), (8, 1)))
    wb = jnp.einsum("Rk,anke->anRe", rep_rows, val, precision=hp)
    wb_shape = (8, q * 128, 256)
    row_gl = (lax.broadcasted_iota(jnp.int32, wb_shape, 1) // SSM_GROUP) % 8
    col_gq = lax.broadcasted_iota(jnp.int32, wb_shape, 2) // SSM_STATE
    tile_nb = lax.broadcasted_iota(jnp.int32, wb_shape, 0)
    wb = jnp.where(row_gl == 4 * (tile_nb % 2) + col_gq, wb, 0.0)
    wb = wb.reshape(16, q * 128, 256).astype(BF16)

    cre = c_re.astype(F32)
    cim = c_im.astype(F32)
    care = jnp.stack([cre * pr[t][:, None, :] - cim * pi[t][:, None, :] for t in range(q + 1)])
    caim = jnp.stack([cre * pi[t][:, None, :] + cim * pr[t][:, None, :] for t in range(q + 1)])
    rep_cols = jnp.tile(jnp.eye(SSM_GROUP, dtype=F32), (1, 16))

    def block_diag(v, per_group):
        out = jnp.einsum("rhkc,cn->rhkn", v, rep_cols, precision=hp)
        row_g = lax.broadcasted_iota(jnp.int32, out.shape, 2) // per_group
        col_g = lax.broadcasted_iota(jnp.int32, out.shape, 3) // SSM_GROUP
        return jnp.where(row_g == col_g, out, 0.0)

    def pack_c(cv):
        cv = cv.reshape(q, 2, 16, SSM_GROUP, SSM_STATE).transpose(0, 1, 2, 4, 3)
        return block_diag(cv.reshape(q, 2, 16 * SSM_STATE, SSM_GROUP), SSM_STATE).astype(BF16)

    wcr = pack_c(care[1:])
    wci = pack_c(-caim[1:])

    kt = (jnp.einsum("tgcp,gpd->tgcd", care[:q], bbr, precision=hp)
          - jnp.einsum("tgcp,gpd->tgcd", caim[:q], bbi, precision=hp))
    kt = kt.reshape(q, 2, 16, SSM_GROUP, SSM_GROUP).transpose(0, 1, 2, 4, 3)
    tt = block_diag(kt.reshape(q, 2, 16 * SSM_GROUP, SSM_GROUP), SSM_GROUP)
    zero = jnp.zeros((2, 256, 256), F32)
    wk = jnp.stack([
        jnp.concatenate([tt[r - s] if s <= r else zero for s in range(q)], axis=1)
        for r in range(q)]).astype(BF16)

    aqr, aqi = powers(pr[q].reshape(1, N_STATE), pi[q].reshape(1, N_STATE), SUBLANES)
    rows = jnp.arange(SUBLANES)[:, None]
    tabs = []
    for d in (1, 2, 4):
        keep = (rows >= d).astype(F32)
        tabs += [keep * aqr[d], keep * aqi[d]]
    tabs += [jnp.concatenate(aqr[1:], axis=0), jnp.concatenate(aqi[1:], axis=0)]
    tab = jnp.stack(tabs)

    d4 = jnp.tile(d_skip.astype(F32).reshape(1, SSM_WIDTH), (1, q))
    return dict(wb=wb, wcr=wcr, wci=wci, wk=wk, tab=tab, d=d4)


def _mix_kernel(n_ref, y_ref, cg_ref, hist_ref, *rest, tt):
    n_gate = 2 * D_MODEL // GATE_BLOCK
    wg_refs = rest[:n_gate]
    (wglu_ref, wsp_ref, dww_ref, dwb_ref, lng_ref, lnb_ref, wcp_ref,
     merged_ref, hist_out_ref, buf, conv_scr, cb_scr, gate_scr) = rest[n_gate:]
    hist_rows = CONV_HIST * SUBLANES

    @pl.when(pl.program_id(1) == 0)
    def _():
        buf[0:hist_rows, :] = hist_ref[...]

    @pl.when(pl.program_id(1) != 0)
    def _():
        buf[0:hist_rows, :] = buf[tt * SUBLANES:tt * SUBLANES + hist_rows, :]

    buf[hist_rows:hist_rows + tt * SUBLANES, :] = cg_ref[0]
    hist_out_ref[0] = buf[tt * SUBLANES:tt * SUBLANES + hist_rows, :]

    tb = 16
    off0 = CONV_HIST - (CONV_K - 1)
    blk = MIX_BLOCK
    n = n_ref[0]

    def zero_of(v):
        u = pltpu.bitcast(v, jnp.uint32)
        return pltpu.bitcast((u >> 16) >> 16, F32)

    def plus(a, b):
        return b if a is None else (a if b is None else a + b)

    def conv_block(base, after):
        bias = dwb_ref[...] if after is None else dwb_ref[...] + after
        for sub in range(base, base + tb, SUBLANES):
            accs = [bias for _ in range(SUBLANES)]
            for k in range(CONV_K):
                wk = dww_ref[k]
                for t in range(SUBLANES):
                    r0 = (sub + t + off0 + k) * SUBLANES
                    accs[t] = accs[t] + buf[r0:r0 + SUBLANES, :] * wk
            for t in range(SUBLANES):
                r0 = (sub + t) * SUBLANES
                conv_scr[r0:r0 + SUBLANES, :] = accs[t]

        c = jnp.concatenate(
            [conv_scr[pl.ds(base * SUBLANES + s, tb, stride=SUBLANES), :] for s in range(SUBLANES)],
            axis=1)
        mu = jnp.mean(c, axis=-1, keepdims=True)
        cen = c - mu
        var = jnp.mean(cen * cen, axis=-1, keepdims=True)
        ln = cen * lax.rsqrt(var + LN_EPS) * lng_ref[...] + lnb_ref[...]
        cb_scr[base:base + tb, :] = (ln * jax.nn.sigmoid(ln)).astype(BF16)
        return zero_of(ln[tb - SUBLANES:tb, CONV_WIDTH - 128:CONV_WIDTH])

    n_conv = tt // tb
    gate_done = [None] * n_gate
    conv_done = [None] * n_gate
    for c in range(n_gate):
        cols = slice(GATE_BLOCK * c, GATE_BLOCK * (c + 1))
        after = conv_done[c - 2] if c >= 2 else None
        if after is None:
            lhs = n
        else:
            lead = n[0:16, 0:128] + jnp.concatenate([after, after], axis=0).astype(BF16)
            lhs = jnp.concatenate([lead, n[0:16, 128:]], axis=1)
            if tt > 16:
                lhs = jnp.concatenate([lhs, n[16:]], axis=0)
        g = _dot(lhs, wg_refs[c][...])
        gate_scr[:, cols] = jax.nn.sigmoid(g)
        gate_done[c] = zero_of(g[tt - SUBLANES:tt, GATE_BLOCK - 128:GATE_BLOCK])
        after = plus(gate_done[c - 1], conv_done[c - 1]) if c >= 1 else None
        done = None
        for cbi in range(c * n_conv // n_gate, (c + 1) * n_conv // n_gate):
            done = plus(done, conv_block(cbi * tb, after))
        conv_done[c] = done if done is not None or c == 0 else conv_done[c - 1]

    y = jnp.concatenate([y_ref[0, cbk] for cbk in range(SSM_WIDTH // 128)], axis=1)
    yg = (y * jax.nn.sigmoid(_dot(y.astype(BF16), wglu_ref[...]))).astype(BF16)
    cb = cb_scr[...]
    for j in range(D_MODEL // blk):
        cols = slice(blk * j, blk * (j + 1))
        gcols = slice(D_MODEL + blk * j, D_MODEL + blk * (j + 1))
        ba = _dot(yg, wsp_ref[:, cols])
        bb = _dot(cb, wcp_ref[:, cols])
        merged_ref[0, :, cols] = (gate_scr[:, cols] * ba + gate_scr[:, gcols] * bb).astype(BF16)


def _mix(n, y, cg, hist, w_in, wglu, wsp, dww, dwb, lng, lnb, wcp, tt):
    b, l, _ = n.shape
    gate_blk0 = (SSM_WIDTH + 2 * CONV_WIDTH) // GATE_BLOCK
    kern = functools.partial(_mix_kernel, tt=tt)
    tok = lambda width: pl.BlockSpec((1, tt, width), lambda bi, i: (bi, i, 0))
    return pl.pallas_call(
        kern,
        grid=(b, l // tt),
        in_specs=[
            tok(D_MODEL),
            pl.BlockSpec((1, SSM_WIDTH // 128, tt, 128), lambda bi, i: (bi, 0, i, 0)),
            pl.BlockSpec((1, tt * SUBLANES, CONV_LANES), lambda bi, i: (bi, i, 0)),
            _const_spec((CONV_HIST * SUBLANES, CONV_LANES)),
            *[pl.BlockSpec((D_MODEL, GATE_BLOCK), lambda bi, i, k=k: (0, gate_blk0 + k),
                           pipeline_mode=pl.Buffered(1))
              for k in range(2 * D_MODEL // GATE_BLOCK)],
            _const_spec((SSM_WIDTH, SSM_WIDTH)),
            _const_spec((SSM_WIDTH, D_MODEL)),
            _const_spec((CONV_K, SUBLANES, CONV_LANES)),
            _const_spec((SUBLANES, CONV_LANES)),
            _const_spec((1, CONV_WIDTH)),
            _const_spec((1, CONV_WIDTH)),
            _const_spec((CONV_WIDTH, D_MODEL)),
        ],
        out_specs=[
            tok(D_MODEL),
            pl.BlockSpec((1, CONV_HIST * SUBLANES, CONV_LANES), lambda bi, i: (bi, 0, 0)),
        ],
        out_shape=[
            jax.ShapeDtypeStruct((b, l, D_MODEL), BF16),
            jax.ShapeDtypeStruct((b, CONV_HIST * SUBLANES, CONV_LANES), F32),
        ],
        scratch_shapes=[
            pltpu.VMEM(((CONV_HIST + tt) * SUBLANES, CONV_LANES), F32),
            pltpu.VMEM((tt * SUBLANES, CONV_LANES), F32),
            pltpu.VMEM((tt, CONV_WIDTH), BF16),
            pltpu.VMEM((tt, 2 * D_MODEL), F32),
        ],
        compiler_params=_params(2, MIX_VMEM_LIMIT),
    )(n, y, cg, hist, *([w_in] * (2 * D_MODEL // GATE_BLOCK)), wglu, wsp,
      dww.reshape(CONV_K, SUBLANES, CONV_LANES), dwb.reshape(SUBLANES, CONV_LANES), lng, lnb, wcp)


def _mix_out_kernel(x_ref, m_ref, w_ref, g_ref, g2_ref, h_ref, n2_ref):
    o = _dot(m_ref[0], w_ref[...])
    h = x_ref[0] + _rms(o, g_ref[...])
    h_ref[0] = h
    n2_ref[0] = _rms(h, g2_ref[...]).astype(BF16)


def _mix_out(x, merged, w, g, g2, tt):
    b, l, _ = x.shape
    tok = pl.BlockSpec((1, tt, D_MODEL), lambda bi, i: (bi, i, 0))
    return pl.pallas_call(
        _mix_out_kernel,
        grid=(b, l // tt),
        in_specs=[tok, tok, _const_spec((D_MODEL, D_MODEL)), _const_spec((1, D_MODEL)),
                  _const_spec((1, D_MODEL))],
        out_specs=[tok, tok],
        out_shape=[jax.ShapeDtypeStruct((b, l, D_MODEL), F32),
                   jax.ShapeDtypeStruct((b, l, D_MODEL), BF16)],
        compiler_params=_params(2),
    )(x, merged, w, g, g2)


def _ffn_kernel(h_hbm, n_ref, hist_ref, wg_ref, wv_ref, dw_ref, b_ref,
                wd_ref, gpost_ref, out_ref, hist_out_ref, ubuf, act_scr, carry, h_buf, h_sem,
                *, tt, nf):
    bi = pl.program_id(0)
    i = pl.program_id(1)
    j = pl.program_id(2)
    fb = FFN_BLOCK

    def residual_copy():
        return pltpu.make_async_copy(h_hbm.at[bi, pl.ds(i * tt, tt), :], h_buf, h_sem)

    @pl.when(j == 0)
    def _():
        residual_copy().start()
        out_ref[0] = jnp.zeros((tt, D_MODEL), F32)

    @pl.when(i == 0)
    def _():
        carry[j] = hist_ref[j]

    ubuf[0:FFN_HIST, :] = carry[j]
    dwg, dwv = dw_ref[j], dw_ref[nf + j]
    taps = [jnp.concatenate([dwg[k:k + 1, :], dwv[k:k + 1, :]], axis=1) for k in range(FFN_K)]
    bias = jnp.concatenate([b_ref[j], b_ref[nf + j]], axis=1)

    slab = min(tt, FFN_SLAB)

    def up(r0):
        n = n_ref[0, r0:r0 + slab, :]
        ubuf[FFN_HIST + r0:FFN_HIST + r0 + slab, 0:fb] = _dot(n, wg_ref[...])
        ubuf[FFN_HIST + r0:FFN_HIST + r0 + slab, fb:2 * fb] = _dot(n, wv_ref[...])

    up(0)
    for r0 in range(0, tt, slab):
        if r0 + slab < tt:
            up(r0 + slab)
        for c0 in range(r0, r0 + slab, FFN_CHUNK):
            win = ubuf[c0:c0 + FFN_HIST + FFN_CHUNK, :]
            conv = bias + win[FFN_HIST:] * taps[FFN_K - 1]
            for k in range(FFN_K - 1):
                d = FFN_K - 1 - k
                conv = conv + pltpu.roll(win, d, 0)[FFN_HIST:] * taps[k]
            act_scr[c0:c0 + FFN_CHUNK, :] = (
                jax.nn.gelu(conv[:, 0:fb]) * conv[:, fb:2 * fb]).astype(BF16)
        out_ref[0, r0:r0 + slab, :] += _dot(act_scr[r0:r0 + slab, :], wd_ref[...])

    tail = ubuf[tt:tt + FFN_HIST, :]
    carry[j] = tail
    hist_out_ref[0, j] = tail

    @pl.when(j == nf - 1)
    def _():
        residual_copy().wait()
        out_ref[0] = h_buf[...] + _rms(out_ref[0], gpost_ref[...])


def _ffn(h, n2, hist, wup, dww, dwb, wdown, gpost, tt):
    b, l, _ = h.shape
    fb = FFN_BLOCK
    nf = D_FF // fb
    kern = functools.partial(_ffn_kernel, tt=tt, nf=nf)
    tok = pl.BlockSpec((1, tt, D_MODEL), lambda bi, i, j: (bi, i, 0))
    return pl.pallas_call(
        kern,
        grid=(b, l // tt, nf),
        in_specs=[
            pl.BlockSpec(memory_space=pl.ANY),
            tok,
            _const_spec((nf, FFN_HIST, 2 * fb)),
            pl.BlockSpec((D_MODEL, fb), lambda bi, i, j: (0, j)),
            pl.BlockSpec((D_MODEL, fb), lambda bi, i, j: (0, nf + j)),
            _const_spec((2 * nf, FFN_K, fb)),
            _const_spec((2 * nf, 1, fb)),
            pl.BlockSpec((fb, D_MODEL), lambda bi, i, j: (j, 0)),
            _const_spec((1, D_MODEL)),
        ],
        out_specs=[
            tok,
            pl.BlockSpec((1, nf, FFN_HIST, 2 * fb), lambda bi, i, j: (bi, 0, 0, 0)),
        ],
        out_shape=[
            jax.ShapeDtypeStruct((b, l, D_MODEL), F32),
            jax.ShapeDtypeStruct((b, nf, FFN_HIST, 2 * fb), F32),
        ],
        scratch_shapes=[
            pltpu.VMEM((FFN_HIST + tt, 2 * fb), F32),
            pltpu.VMEM((tt, fb), BF16),
            pltpu.VMEM((nf, FFN_HIST, 2 * fb), F32),
            pltpu.VMEM((tt, D_MODEL), F32),
            pltpu.SemaphoreType.DMA(()),
        ],
        compiler_params=_params(3, FFN_VMEM_LIMIT),
    )(h, n2, hist, wup, wup, dww, dwb, wdown, gpost)


def _block(x, carries, p, ssm, tiles):
    b, l, _ = x.shape
    q = SSM_Q
    tt_in, m_ssm, tt_mix, tt_out, tt_ffn = tiles
    state0, conv_hist0, ffn_hist0 = carries

    n, u4, cg = _in_proj(x, p["g_mix_pre"], p["w_in"], tt_in)

    rows = l // q
    pad = (-rows) % m_ssm
    if pad:
        u4 = jnp.pad(u4, ((0, 0), (0, pad), (0, 0)))
    last_row = (rows - 1) % SUBLANES
    y, state = _ssm(u4, state0, ssm, m_ssm, last_row)
    if pad:
        y = y[:, :, :l]

    merged, conv_hist = _mix(n, y, cg, conv_hist0, p["w_in"], p["w_glu"],
                             p["w_sp"], p["conv_w"], p["conv_b"], p["ln_g"], p["ln_b"],
                             p["w_cp"], tt_mix)
    h1, n2 = _mix_out(x, merged, p["w_mo"], p["g_mix_post"], p["g_ffn_pre"], tt_out)
    out, ffn_hist = _ffn(h1, n2, ffn_hist0, p["w_up"], p["ffn_w"], p["ffn_b"],
                         p["w_down"], p["g_ffn_post"], tt_ffn)
    return out, (state[0], conv_hist[0], ffn_hist[0])


def kernel(x, meta_tokens, norm_mix_pre, w_in, lam_re, lam_im, log_step, ssm_b_re, ssm_b_im,
           ssm_c_re, ssm_c_im, ssm_d, w_ssm_glu, w_ssm_proj, conv_dw_w, conv_dw_b, conv_ln_g,
           conv_ln_b, w_conv_proj, w_mix_out, norm_mix_post, norm_ffn_pre, w_ffn_up, ffn_dw_w,
           ffn_dw_b, w_ffn_down, norm_ffn_post):
    depth = w_in.shape[0]
    nf = D_FF // FFN_BLOCK
    zero_carries = (
        jnp.zeros((SUBLANES, 2 * N_STATE), F32),
        jnp.zeros((CONV_HIST * SUBLANES, CONV_LANES), F32),
        jnp.zeros((nf, FFN_HIST, 2 * FFN_BLOCK), F32),
    )
    h_meta = meta_tokens.astype(x.dtype)[None]
    h = x
    row = lambda v: v.astype(F32).reshape(1, -1)
    blocked = lambda v: v.reshape(v.shape[0], 2 * nf, FFN_BLOCK).transpose(1, 0, 2)
    for i in range(depth):
        p = dict(
            g_mix_pre=row(norm_mix_pre[i]),
            w_in=w_in[i].astype(BF16),
            w_glu=w_ssm_glu[i].astype(BF16),
            w_sp=w_ssm_proj[i].astype(BF16),
            conv_w=conv_dw_w[i].astype(F32),
            conv_b=row(conv_dw_b[i]),
            ln_g=row(conv_ln_g[i]),
            ln_b=row(conv_ln_b[i]),
            w_cp=w_conv_proj[i].astype(BF16),
            w_mo=w_mix_out[i].astype(BF16),
            g_mix_post=row(norm_mix_post[i]),
            g_ffn_pre=row(norm_ffn_pre[i]),
            w_up=w_ffn_up[i].astype(BF16),
            ffn_w=blocked(ffn_dw_w[i].astype(F32)),
            ffn_b=blocked(row(ffn_dw_b[i])),
            w_down=w_ffn_down[i].astype(BF16),
            g_ffn_post=row(norm_ffn_post[i]),
        )
        ssm = _ssm_prep(lam_re[i], lam_im[i], log_step[i], ssm_b_re[i], ssm_b_im[i],
                        ssm_c_re[i], ssm_c_im[i], ssm_d[i])
        h_meta, carries = _block(h_meta, zero_carries, p, ssm,
                                 (N_META, SUBLANES, N_META, N_META, N_META))
        h, _ = _block(h, carries, p, ssm, (512, 512, 512, 512, 1024))
    return h
```

```python
import functools

import jax
import jax.numpy as jnp
from jax import lax
from jax.experimental import pallas as pl
from jax.experimental.pallas import tpu as pltpu

F32 = jnp.float32
BF16 = jnp.bfloat16

D_MODEL = 2048
N_META = 16
SSM_WIDTH = 512
SSM_GROUP = 16
SSM_GROUPS = 32
SSM_STATE = 64
N_STATE = SSM_GROUPS * SSM_STATE
CONV_WIDTH = 1024
CONV_K = 31
D_FF = 5632
FFN_K = 3
NORM_EPS = 1e-6
LN_EPS = 1e-5

SSM_Q = 4
CONV_HIST = 32
FFN_HIST = 8
MIX_BLOCK = 256
GATE_BLOCK = 256
FFN_BLOCK = 512
FFN_SLAB = 512
FFN_CHUNK = 16
SUBLANES = 8
CONV_LANES = CONV_WIDTH // SUBLANES

VMEM_LIMIT = 56 * 1024 * 1024
MIX_VMEM_LIMIT = 60 * 1024 * 1024
FFN_VMEM_LIMIT = 62 * 1024 * 1024


def _params(n_axes, vmem_limit=VMEM_LIMIT):
    return pltpu.CompilerParams(
        dimension_semantics=("arbitrary",) * n_axes,
        vmem_limit_bytes=vmem_limit)


def _const_spec(shape):
    zeros = (0,) * len(shape)
    return pl.BlockSpec(shape, lambda *_: zeros, pipeline_mode=pl.Buffered(1))


def _rms(x, g):
    ms = jnp.mean(x * x, axis=-1, keepdims=True)
    return x * lax.rsqrt(ms + NORM_EPS) * g


def _dot(a, b):
    return jnp.dot(a, b, preferred_element_type=F32)


def _in_proj_kernel(x_ref, g_ref, w_ref, n_ref, u_ref, cg_ref, u_scr, *, tt):
    q = SSM_Q
    lanes = 128
    n = _rms(x_ref[0], g_ref[...]).astype(BF16)
    n_ref[0] = n
    u = _dot(n, w_ref[:, 0:SSM_WIDTH])
    for cb in range(SSM_WIDTH // lanes):
        u_scr[cb] = u[:, lanes * cb:lanes * (cb + 1)]
    for s in range(q):
        for cb in range(SSM_WIDTH // lanes):
            c0 = SSM_WIDTH * s + lanes * cb
            u_ref[0, :, c0:c0 + lanes] = u_scr[cb, pl.ds(s, tt // q, stride=q), :]
    blk = GATE_BLOCK
    for j in range(CONV_WIDTH // blk):
        v0 = SSM_WIDTH + blk * j
        g0 = SSM_WIDTH + CONV_WIDTH + blk * j
        val = _dot(n, w_ref[:, v0:v0 + blk])
        gate = _dot(n, w_ref[:, g0:g0 + blk])
        cg = val * jax.nn.sigmoid(gate)
        for e in range(blk // lanes):
            s = j * (blk // lanes) + e
            cg_ref[0, pl.ds(s, tt, stride=SUBLANES), :] = cg[:, lanes * e:lanes * (e + 1)]


def _in_proj(x, g, w, tt):
    b, l, _ = x.shape
    ncol = SSM_WIDTH + 2 * CONV_WIDTH
    q = SSM_Q
    return pl.pallas_call(
        functools.partial(_in_proj_kernel, tt=tt),
        grid=(b, l // tt),
        in_specs=[
            pl.BlockSpec((1, tt, D_MODEL), lambda bi, i: (bi, i, 0)),
            _const_spec((1, D_MODEL)),
            _const_spec((D_MODEL, ncol)),
        ],
        out_specs=[
            pl.BlockSpec((1, tt, D_MODEL), lambda bi, i: (bi, i, 0)),
            pl.BlockSpec((1, tt // q, q * SSM_WIDTH), lambda bi, i: (bi, i, 0)),
            pl.BlockSpec((1, tt * SUBLANES, CONV_LANES), lambda bi, i: (bi, i, 0)),
        ],
        out_shape=[
            jax.ShapeDtypeStruct((b, l, D_MODEL), BF16),
            jax.ShapeDtypeStruct((b, l // q, q * SSM_WIDTH), F32),
            jax.ShapeDtypeStruct((b, l * SUBLANES, CONV_LANES), F32),
        ],
        scratch_shapes=[pltpu.VMEM((SSM_WIDTH // 128, tt, 128), F32)],
        compiler_params=_params(2),
    )(x, g, w)


def _ssm_kernel(u_ref, init_ref, tab_ref, wb_ref, wcr_ref, wci_ref, wk_ref, d_ref,
                y_ref, state_out_ref, x_scr, state_scr, *, m, last_row):
    q = SSM_Q
    lane_blk = 128
    tile_n = 256

    @pl.when(pl.program_id(1) == 0)
    def _():
        state_scr[...] = init_ref[...]

    u = u_ref[0]
    ub = u.astype(BF16)

    for n in range(2 * N_STATE // tile_n):
        kb = (n % (N_STATE // tile_n)) // 2
        lhs = jnp.concatenate(
            [ub[:, SSM_WIDTH * s + lane_blk * kb:SSM_WIDTH * s + lane_blk * (kb + 1)]
             for s in range(q)], axis=1)
        x_scr[:, tile_n * n:tile_n * (n + 1)] = _dot(lhs, wb_ref[n])

    w = 512
    row_id = lax.broadcasted_iota(jnp.int32, (SUBLANES, w), 0)
    for j in range(N_STATE // w):
        sr = slice(w * j, w * (j + 1))
        si = slice(N_STATE + w * j, N_STATE + w * (j + 1))
        tabs = [tab_ref[t, :, sr] for t in range(8)]

        cr, ci = state_scr[:, sr], state_scr[:, si]
        for rb in range(m // SUBLANES):
            rows = slice(rb * SUBLANES, (rb + 1) * SUBLANES)
            xr = x_scr[rows, sr]
            xi = x_scr[rows, si]
            for lvl, d in enumerate((1, 2, 4)):
                ar, ai = tabs[2 * lvl], tabs[2 * lvl + 1]
                pr = pltpu.roll(xr, d, 0)
                pi = pltpu.roll(xi, d, 0)
                xr, xi = xr + ar * pr - ai * pi, xi + ar * pi + ai * pr
            ar, ai = tabs[6], tabs[7]
            xr, xi = xr + ar * cr - ai * ci, xi + ar * ci + ai * cr
            x_scr[rows, sr] = jnp.where(row_id == 0, cr, pltpu.roll(xr, 1, 0))
            x_scr[rows, si] = jnp.where(row_id == 0, ci, pltpu.roll(xi, 1, 0))
            if rb == m // SUBLANES - 1:
                state_scr[:, sr] = jnp.broadcast_to(xr[last_row:last_row + 1], (SUBLANES, w))
                state_scr[:, si] = jnp.broadcast_to(xi[last_row:last_row + 1], (SUBLANES, w))
            cr = jnp.broadcast_to(xr[SUBLANES - 1:SUBLANES], (SUBLANES, w))
            ci = jnp.broadcast_to(xi[SUBLANES - 1:SUBLANES], (SUBLANES, w))

    state_out_ref[0] = state_scr[...]

    half = 256
    st_per_half = half // SSM_GROUP * SSM_STATE
    for h in range(SSM_WIDTH // half):
        er = x_scr[:, st_per_half * h:st_per_half * (h + 1)].astype(BF16)
        ei = x_scr[:, N_STATE + st_per_half * h:N_STATE + st_per_half * (h + 1)].astype(BF16)
        for r in range(q):
            lanes = slice(SSM_WIDTH * r + half * h, SSM_WIDTH * r + half * (h + 1))
            lhs = jnp.concatenate(
                [ub[:, SSM_WIDTH * s + half * h:SSM_WIDTH * s + half * (h + 1)]
                 for s in range(r + 1)], axis=1)
            acc = _dot(er, wcr_ref[r, h]) + _dot(ei, wci_ref[r, h])
            acc = acc + _dot(lhs, wk_ref[r, h, 0:half * (r + 1), :])
            y = jax.nn.gelu(acc + d_ref[:, lanes] * u[:, lanes])
            for e in range(half // lane_blk):
                y_ref[0, (half // lane_blk) * h + e, pl.ds(r, m, stride=q), :] = (
                    y[:, lane_blk * e:lane_blk * (e + 1)])


def _ssm(u4, init_state, prep, m, last_row):
    b, rows, width = u4.shape
    q = SSM_Q
    kern = functools.partial(_ssm_kernel, m=m, last_row=last_row)
    return pl.pallas_call(
        kern,
        grid=(b, rows // m),
        in_specs=[
            pl.BlockSpec((1, m, width), lambda bi, i: (bi, i, 0)),
            _const_spec((SUBLANES, 2 * N_STATE)),
            _const_spec((8, SUBLANES, N_STATE)),
            _const_spec((16, q * 128, 256)),
            _const_spec((q, 2, 1024, 256)),
            _const_spec((q, 2, 1024, 256)),
            _const_spec((q, 2, q * 256, 256)),
            _const_spec((1, width)),
        ],
        out_specs=[
            pl.BlockSpec((1, SSM_WIDTH // 128, q * m, 128), lambda bi, i: (bi, 0, i, 0)),
            pl.BlockSpec((1, SUBLANES, 2 * N_STATE), lambda bi, i: (bi, 0, 0)),
        ],
        out_shape=[
            jax.ShapeDtypeStruct((b, SSM_WIDTH // 128, q * rows, 128), F32),
            jax.ShapeDtypeStruct((b, SUBLANES, 2 * N_STATE), F32),
        ],
        scratch_shapes=[
            pltpu.VMEM((m, 2 * N_STATE), F32),
            pltpu.VMEM((SUBLANES, 2 * N_STATE), F32),
        ],
        compiler_params=_params(2),
    )(u4, init_state, prep["tab"], prep["wb"], prep["wcr"], prep["wci"], prep["wk"], prep["d"])


def _ssm_prep(lam_re, lam_im, log_step, b_re, b_im, c_re, c_im, d_skip):
    q = SSM_Q
    hp = lax.Precision.HIGHEST
    lr = lam_re.astype(F32)
    li = lam_im.astype(F32)
    step = jnp.exp(log_step.astype(F32))[:, None]
    mag = jnp.exp(lr * step)
    ar = mag * jnp.cos(li * step)
    ai = mag * jnp.sin(li * step)
    den = lr * lr + li * li
    cr = ((ar - 1.0) * lr + ai * li) / den
    ci = (ai * lr - (ar - 1.0) * li) / den
    br_ = b_re.astype(F32)
    bi_ = b_im.astype(F32)
    bbr = cr[..., None] * br_ - ci[..., None] * bi_
    bbi = cr[..., None] * bi_ + ci[..., None] * br_

    def powers(xr, xi, n):
        outr, outi = [jnp.ones_like(xr)], [jnp.zeros_like(xi)]
        for _ in range(n):
            pr, pi = outr[-1], outi[-1]
            outr.append(pr * xr - pi * xi)
            outi.append(pr * xi + pi * xr)
        return outr, outi

    pr, pi = powers(ar, ai, q)

    vre = jnp.stack([pr[q - 1 - s][..., None] * bbr - pi[q - 1 - s][..., None] * bbi for s in range(q)])
    vim = jnp.stack([pr[q - 1 - s][..., None] * bbi + pi[q - 1 - s][..., None] * bbr for s in range(q)])
    val = jnp.stack([vre, vim])
    val = val.reshape(2, q, 8, 4, SSM_STATE, SSM_GROUP).transpose(0, 2, 1, 5, 3, 4)
    val = val.reshape(2, 8, q * SSM_GROUP, 256)
    rep_rows = jnp.kron(jnp.eye(q, dtype=F32), jnp.tile(jnp.eye(SSM_GROUP, dtype=F32), (8, 1)))
    wb = jnp.einsum("Rk,anke->anRe", rep_rows, val, precision=hp)
    wb_shape = (8, q * 128, 256)
    row_gl = (lax.broadcasted_iota(jnp.int32, wb_shape, 1) // SSM_GROUP) % 8
    col_gq = lax.broadcasted_iota(jnp.int32, wb_shape, 2) // SSM_STATE
    tile_nb = lax.broadcasted_iota(jnp.int32, wb_shape, 0)
    wb = jnp.where(row_gl == 4 * (tile_nb % 2) + col_gq, wb, 0.0)
    wb = wb.reshape(16, q * 128, 256).astype(BF16)

    cre = c_re.astype(F32)
    cim = c_im.astype(F32)
    care = jnp.stack([cre * pr[t][:, None, :] - cim * pi[t][:, None, :] for t in range(q + 1)])
    caim = jnp.stack([cre * pi[t][:, None, :] + cim * pr[t][:, None, :] for t in range(q + 1)])
    rep_cols = jnp.tile(jnp.eye(SSM_GROUP, dtype=F32), (1, 16))

    def block_diag(v, per_group):
        out = jnp.einsum("rhkc,cn->rhkn", v, rep_cols, precision=hp)
        row_g = lax.broadcasted_iota(jnp.int32, out.shape, 2) // per_group
        col_g = lax.broadcasted_iota(jnp.int32, out.shape, 3) // SSM_GROUP
        return jnp.where(row_g == col_g, out, 0.0)

    def pack_c(cv):
        cv = cv.reshape(q, 2, 16, SSM_GROUP, SSM_STATE).transpose(0, 1, 2, 4, 3)
        return block_diag(cv.reshape(q, 2, 16 * SSM_STATE, SSM_GROUP), SSM_STATE).astype(BF16)

    wcr = pack_c(care[1:])
    wci = pack_c(-caim[1:])

    kt = (jnp.einsum("tgcp,gpd->tgcd", care[:q], bbr, precision=hp)
          - jnp.einsum("tgcp,gpd->tgcd", caim[:q], bbi, precision=hp))
    kt = kt.reshape(q, 2, 16, SSM_GROUP, SSM_GROUP).transpose(0, 1, 2, 4, 3)
    tt = block_diag(kt.reshape(q, 2, 16 * SSM_GROUP, SSM_GROUP), SSM_GROUP)
    zero = jnp.zeros((2, 256, 256), F32)
    wk = jnp.stack([
        jnp.concatenate([tt[r - s] if s <= r else zero for s in range(q)], axis=1)
        for r in range(q)]).astype(BF16)

    aqr, aqi = powers(pr[q].reshape(1, N_STATE), pi[q].reshape(1, N_STATE), SUBLANES)
    rows = jnp.arange(SUBLANES)[:, None]
    tabs = []
    for d in (1, 2, 4):
        keep = (rows >= d).astype(F32)
        tabs += [keep * aqr[d], keep * aqi[d]]
    tabs += [jnp.concatenate(aqr[1:], axis=0), jnp.concatenate(aqi[1:], axis=0)]
    tab = jnp.stack(tabs)

    d4 = jnp.tile(d_skip.astype(F32).reshape(1, SSM_WIDTH), (1, q))
    return dict(wb=wb, wcr=wcr, wci=wci, wk=wk, tab=tab, d=d4)


def _mix_kernel(n_ref, y_ref, cg_ref, hist_ref, *rest, tt):
    n_gate = 2 * D_MODEL // GATE_BLOCK
    wg_refs = rest[:n_gate]
    (wglu_ref, wsp_ref, dww_ref, dwb_ref, lng_ref, lnb_ref, wcp_ref,
     merged_ref, hist_out_ref, buf, conv_scr, cb_scr, gate_scr) = rest[n_gate:]
    hist_rows = CONV_HIST * SUBLANES

    @pl.when(pl.program_id(1) == 0)
    def _():
        buf[0:hist_rows, :] = hist_ref[...]

    @pl.when(pl.program_id(1) != 0)
    def _():
        buf[0:hist_rows, :] = buf[tt * SUBLANES:tt * SUBLANES + hist_rows, :]

    buf[hist_rows:hist_rows + tt * SUBLANES, :] = cg_ref[0]
    hist_out_ref[0] = buf[tt * SUBLANES:tt * SUBLANES + hist_rows, :]

    tb = 16
    off0 = CONV_HIST - (CONV_K - 1)
    blk = MIX_BLOCK
    n = n_ref[0]

    def zero_of(v):
        u = pltpu.bitcast(v, jnp.uint32)
        return pltpu.bitcast((u >> 16) >> 16, F32)

    def plus(a, b):
        return b if a is None else (a if b is None else a + b)

    def conv_block(base, after):
        bias = dwb_ref[...] if after is None else dwb_ref[...] + after
        for sub in range(base, base + tb, SUBLANES):
            accs = [bias for _ in range(SUBLANES)]
            for k in range(CONV_K):
                wk = dww_ref[k]
                for t in range(SUBLANES):
                    r0 = (sub + t + off0 + k) * SUBLANES
                    accs[t] = accs[t] + buf[r0:r0 + SUBLANES, :] * wk
            for t in range(SUBLANES):
                r0 = (sub + t) * SUBLANES
                conv_scr[r0:r0 + SUBLANES, :] = accs[t]

        c = jnp.concatenate(
            [conv_scr[pl.ds(base * SUBLANES + s, tb, stride=SUBLANES), :] for s in range(SUBLANES)],
            axis=1)
        mu = jnp.mean(c, axis=-1, keepdims=True)
        cen = c - mu
        var = jnp.mean(cen * cen, axis=-1, keepdims=True)
        ln = cen * lax.rsqrt(var + LN_EPS) * lng_ref[...] + lnb_ref[...]
        cb_scr[base:base + tb, :] = (ln * jax.nn.sigmoid(ln)).astype(BF16)
        return zero_of(ln[tb - SUBLANES:tb, CONV_WIDTH - 128:CONV_WIDTH])

    n_conv = tt // tb
    gate_done = [None] * n_gate
    conv_done = [None] * n_gate
    for c in range(n_gate):
        cols = slice(GATE_BLOCK * c, GATE_BLOCK * (c + 1))
        after = conv_done[c - 2] if c >= 2 else None
        if after is None:
            lhs = n
        else:
            lead = n[0:16, 0:128] + jnp.concatenate([after, after], axis=0).astype(BF16)
            lhs = jnp.concatenate([lead, n[0:16, 128:]], axis=1)
            if tt > 16:
                lhs = jnp.concatenate([lhs, n[16:]], axis=0)
        g = _dot(lhs, wg_refs[c][...])
        gate_scr[:, cols] = jax.nn.sigmoid(g)
        gate_done[c] = zero_of(g[tt - SUBLANES:tt, GATE_BLOCK - 128:GATE_BLOCK])
        after = plus(gate_done[c - 1], conv_done[c - 1]) if c >= 1 else None
        done = None
        for cbi in range(c * n_conv // n_gate, (c + 1) * n_conv // n_gate):
            done = plus(done, conv_block(cbi * tb, after))
        conv_done[c] = done if done is not None or c == 0 else conv_done[c - 1]

    y = jnp.concatenate([y_ref[0, cbk] for cbk in range(SSM_WIDTH // 128)], axis=1)
    yg = (y * jax.nn.sigmoid(_dot(y.astype(BF16), wglu_ref[...]))).astype(BF16)
    cb = cb_scr[...]
    for j in range(D_MODEL // blk):
        cols = slice(blk * j, blk * (j + 1))
        gcols = slice(D_MODEL + blk * j, D_MODEL + blk * (j + 1))
        ba = _dot(yg, wsp_ref[:, cols])
        bb = _dot(cb, wcp_ref[:, cols])
        merged_ref[0, :, cols] = (gate_scr[:, cols] * ba + gate_scr[:, gcols] * bb).astype(BF16)


def _mix(n, y, cg, hist, w_in, wglu, wsp, dww, dwb, lng, lnb, wcp, tt):
    b, l, _ = n.shape
    gate_blk0 = (SSM_WIDTH + 2 * CONV_WIDTH) // GATE_BLOCK
    kern = functools.partial(_mix_kernel, tt=tt)
    tok = lambda width: pl.BlockSpec((1, tt, width), lambda bi, i: (bi, i, 0))
    return pl.pallas_call(
        kern,
        grid=(b, l // tt),
        in_specs=[
            tok(D_MODEL),
            pl.BlockSpec((1, SSM_WIDTH // 128, tt, 128), lambda bi, i: (bi, 0, i, 0)),
            pl.BlockSpec((1, tt * SUBLANES, CONV_LANES), lambda bi, i: (bi, i, 0)),
            _const_spec((CONV_HIST * SUBLANES, CONV_LANES)),
            *[pl.BlockSpec((D_MODEL, GATE_BLOCK), lambda bi, i, k=k: (0, gate_blk0 + k),
                           pipeline_mode=pl.Buffered(1))
              for k in range(2 * D_MODEL // GATE_BLOCK)],
            _const_spec((SSM_WIDTH, SSM_WIDTH)),
            _const_spec((SSM_WIDTH, D_MODEL)),
            _const_spec((CONV_K, SUBLANES, CONV_LANES)),
            _const_spec((SUBLANES, CONV_LANES)),
            _const_spec((1, CONV_WIDTH)),
            _const_spec((1, CONV_WIDTH)),
            _const_spec((CONV_WIDTH, D_MODEL)),
        ],
        out_specs=[
            tok(D_MODEL),
            pl.BlockSpec((1, CONV_HIST * SUBLANES, CONV_LANES), lambda bi, i: (bi, 0, 0)),
        ],
        out_shape=[
            jax.ShapeDtypeStruct((b, l, D_MODEL), BF16),
            jax.ShapeDtypeStruct((b, CONV_HIST * SUBLANES, CONV_LANES), F32),
        ],
        scratch_shapes=[
            pltpu.VMEM(((CONV_HIST + tt) * SUBLANES, CONV_LANES), F32),
            pltpu.VMEM((tt * SUBLANES, CONV_LANES), F32),
            pltpu.VMEM((tt, CONV_WIDTH), BF16),
            pltpu.VMEM((tt, 2 * D_MODEL), F32),
        ],
        compiler_params=_params(2, MIX_VMEM_LIMIT),
    )(n, y, cg, hist, *([w_in] * (2 * D_MODEL // GATE_BLOCK)), wglu, wsp,
      dww.reshape(CONV_K, SUBLANES, CONV_LANES), dwb.reshape(SUBLANES, CONV_LANES), lng, lnb, wcp)


def _mix_out_kernel(x_ref, m_ref, w_ref, g_ref, g2_ref, h_ref, n2_ref):
    o = _dot(m_ref[0], w_ref[...])
    h = x_ref[0] + _rms(o, g_ref[...])
    h_ref[0] = h
    n2_ref[0] = _rms(h, g2_ref[...]).astype(BF16)


def _mix_out(x, merged, w, g, g2, tt):
    b, l, _ = x.shape
    tok = pl.BlockSpec((1, tt, D_MODEL), lambda bi, i: (bi, i, 0))
    return pl.pallas_call(
        _mix_out_kernel,
        grid=(b, l // tt),
        in_specs=[tok, tok, _const_spec((D_MODEL, D_MODEL)), _const_spec((1, D_MODEL)),
                  _const_spec((1, D_MODEL))],
        out_specs=[tok, tok],
        out_shape=[jax.ShapeDtypeStruct((b, l, D_MODEL), F32),
                   jax.ShapeDtypeStruct((b, l, D_MODEL), BF16)],
        compiler_params=_params(2),
    )(x, merged, w, g, g2)


def _ffn_kernel(h_hbm, n_ref, hist_ref, wg_ref, wv_ref, dw_ref, b_ref,
                wd_ref, gpost_ref, out_ref, hist_out_ref, ubuf, act_scr, carry, h_buf, h_sem,
                *, tt, nf):
    bi = pl.program_id(0)
    i = pl.program_id(1)
    j = pl.program_id(2)
    fb = FFN_BLOCK

    def residual_copy():
        return pltpu.make_async_copy(h_hbm.at[bi, pl.ds(i * tt, tt), :], h_buf, h_sem)

    @pl.when(j == 0)
    def _():
        residual_copy().start()
        out_ref[0] = jnp.zeros((tt, D_MODEL), F32)

    @pl.when(i == 0)
    def _():
        carry[j] = hist_ref[j]

    ubuf[0:FFN_HIST, :] = carry[j]
    dwg, dwv = dw_ref[j], dw_ref[nf + j]
    taps = [jnp.concatenate([dwg[k:k + 1, :], dwv[k:k + 1, :]], axis=1) for k in range(FFN_K)]
    bias = jnp.concatenate([b_ref[j], b_ref[nf + j]], axis=1)

    slab = min(tt, FFN_SLAB)

    def up(r0):
        n = n_ref[0, r0:r0 + slab, :]
        ubuf[FFN_HIST + r0:FFN_HIST + r0 + slab, 0:fb] = _dot(n, wg_ref[...])
        ubuf[FFN_HIST + r0:FFN_HIST + r0 + slab, fb:2 * fb] = _dot(n, wv_ref[...])

    up(0)
    for r0 in range(0, tt, slab):
        if r0 + slab < tt:
            up(r0 + slab)
        for c0 in range(r0, r0 + slab, FFN_CHUNK):
            win = ubuf[c0:c0 + FFN_HIST + FFN_CHUNK, :]
            conv = bias + win[FFN_HIST:] * taps[FFN_K - 1]
            for k in range(FFN_K - 1):
                d = FFN_K - 1 - k
                conv = conv + pltpu.roll(win, d, 0)[FFN_HIST:] * taps[k]
            act_scr[c0:c0 + FFN_CHUNK, :] = (
                jax.nn.gelu(conv[:, 0:fb]) * conv[:, fb:2 * fb]).astype(BF16)
        out_ref[0, r0:r0 + slab, :] += _dot(act_scr[r0:r0 + slab, :], wd_ref[...])

    tail = ubuf[tt:tt + FFN_HIST, :]
    carry[j] = tail
    hist_out_ref[0, j] = tail

    @pl.when(j == nf - 1)
    def _():
        residual_copy().wait()
        out_ref[0] = h_buf[...] + _rms(out_ref[0], gpost_ref[...])


def _ffn(h, n2, hist, wup, dww, dwb, wdown, gpost, tt):
    b, l, _ = h.shape
    fb = FFN_BLOCK
    nf = D_FF // fb
    kern = functools.partial(_ffn_kernel, tt=tt, nf=nf)
    tok = pl.BlockSpec((1, tt, D_MODEL), lambda bi, i, j: (bi, i, 0))
    return pl.pallas_call(
        kern,
        grid=(b, l // tt, nf),
        in_specs=[
            pl.BlockSpec(memory_space=pl.ANY),
            tok,
            _const_spec((nf, FFN_HIST, 2 * fb)),
            pl.BlockSpec((D_MODEL, fb), lambda bi, i, j: (0, j)),
            pl.BlockSpec((D_MODEL, fb), lambda bi, i, j: (0, nf + j)),
            _const_spec((2 * nf, FFN_K, fb)),
            _const_spec((2 * nf, 1, fb)),
            pl.BlockSpec((fb, D_MODEL), lambda bi, i, j: (j, 0)),
            _const_spec((1, D_MODEL)),
        ],
        out_specs=[
            tok,
            pl.BlockSpec((1, nf, FFN_HIST, 2 * fb), lambda bi, i, j: (bi, 0, 0, 0)),
        ],
        out_shape=[
            jax.ShapeDtypeStruct((b, l, D_MODEL), F32),
            jax.ShapeDtypeStruct((b, nf, FFN_HIST, 2 * fb), F32),
        ],
        scratch_shapes=[
            pltpu.VMEM((FFN_HIST + tt, 2 * fb), F32),
            pltpu.VMEM((tt, fb), BF16),
            pltpu.VMEM((nf, FFN_HIST, 2 * fb), F32),
            pltpu.VMEM((tt, D_MODEL), F32),
            pltpu.SemaphoreType.DMA(()),
        ],
        compiler_params=_params(3, FFN_VMEM_LIMIT),
    )(h, n2, hist, wup, wup, dww, dwb, wdown, gpost)


def _block(x, carries, p, ssm, tiles):
    b, l, _ = x.shape
    q = SSM_Q
    tt_in, m_ssm, tt_mix, tt_out, tt_ffn = tiles
    state0, conv_hist0, ffn_hist0 = carries

    n, u4, cg = _in_proj(x, p["g_mix_pre"], p["w_in"], tt_in)

    rows = l // q
    pad = (-rows) % m_ssm
    if pad:
        u4 = jnp.pad(u4, ((0, 0), (0, pad), (0, 0)))
    last_row = (rows - 1) % SUBLANES
    y, state = _ssm(u4, state0, ssm, m_ssm, last_row)
    if pad:
        y = y[:, :, :l]

    merged, conv_hist = _mix(n, y, cg, conv_hist0, p["w_in"], p["w_glu"],
                             p["w_sp"], p["conv_w"], p["conv_b"], p["ln_g"], p["ln_b"],
                             p["w_cp"], tt_mix)
    h1, n2 = _mix_out(x, merged, p["w_mo"], p["g_mix_post"], p["g_ffn_pre"], tt_out)
    out, ffn_hist = _ffn(h1, n2, ffn_hist0, p["w_up"], p["ffn_w"], p["ffn_b"],
                         p["w_down"], p["g_ffn_post"], tt_ffn)
    return out, (state[0], conv_hist[0], ffn_hist[0])


def kernel(x, meta_tokens, norm_mix_pre, w_in, lam_re, lam_im, log_step, ssm_b_re, ssm_b_im,
           ssm_c_re, ssm_c_im, ssm_d, w_ssm_glu, w_ssm_proj, conv_dw_w, conv_dw_b, conv_ln_g,
           conv_ln_b, w_conv_proj, w_mix_out, norm_mix_post, norm_ffn_pre, w_ffn_up, ffn_dw_w,
           ffn_dw_b, w_ffn_down, norm_ffn_post):
    depth = w_in.shape[0]
    nf = D_FF // FFN_BLOCK
    zero_carries = (
        jnp.zeros((SUBLANES, 2 * N_STATE), F32),
        jnp.zeros((CONV_HIST * SUBLANES, CONV_LANES), F32),
        jnp.zeros((nf, FFN_HIST, 2 * FFN_BLOCK), F32),
    )
    h_meta = meta_tokens.astype(x.dtype)[None]
    h = x
    row = lambda v: v.astype(F32).reshape(1, -1)
    blocked = lambda v: v.reshape(v.shape[0], 2 * nf, FFN_BLOCK).transpose(1, 0, 2)
    for i in range(depth):
        p = dict(
            g_mix_pre=row(norm_mix_pre[i]),
            w_in=w_in[i].astype(BF16),
            w_glu=w_ssm_glu[i].astype(BF16),
            w_sp=w_ssm_proj[i].astype(BF16),
            conv_w=conv_dw_w[i].astype(F32),
            conv_b=row(conv_dw_b[i]),
            ln_g=row(conv_ln_g[i]),
            ln_b=row(conv_ln_b[i]),
            w_cp=w_conv_proj[i].astype(BF16),
            w_mo=w_mix_out[i].astype(BF16),
            g_mix_post=row(norm_mix_post[i]),
            g_ffn_pre=row(norm_ffn_pre[i]),
            w_up=w_ffn_up[i].astype(BF16),
            ffn_w=blocked(ffn_dw_w[i].astype(F32)),
            ffn_b=blocked(row(ffn_dw_b[i])),
            w_down=w_ffn_down[i].astype(BF16),
            g_ffn_post=row(norm_ffn_post[i]),
        )
        ssm = _ssm_prep(lam_re[i], lam_im[i], log_step[i], ssm_b_re[i], ssm_b_im[i],
                        ssm_c_re[i], ssm_c_im[i], ssm_d[i])
        h_meta, carries = _block(h_meta, zero_carries, p, ssm,
                                 (N_META, SUBLANES, N_META, N_META, N_META))
        h, _ = _block(h, carries, p, ssm, (512, 512, 512, 512, 1024))
    return h
```

```python
import functools

import jax
import jax.numpy as jnp
from jax import lax
from jax.experimental import pallas as pl
from jax.experimental.pallas import tpu as pltpu

F32 = jnp.float32
BF16 = jnp.bfloat16

D_MODEL = 2048
N_META = 16
SSM_WIDTH = 512
SSM_GROUP = 16
SSM_GROUPS = 32
SSM_STATE = 64
N_STATE = SSM_GROUPS * SSM_STATE
CONV_WIDTH = 1024
CONV_K = 31
D_FF = 5632
FFN_K = 3
NORM_EPS = 1e-6
LN_EPS = 1e-5

SSM_Q = 4
CONV_HIST = 32
FFN_HIST = 8
MIX_BLOCK = 256
GATE_BLOCK = 256
FFN_BLOCK = 512
FFN_SLAB = 512
FFN_CHUNK = 16
SUBLANES = 8
CONV_LANES = CONV_WIDTH // SUBLANES

VMEM_LIMIT = 56 * 1024 * 1024
MIX_VMEM_LIMIT = 60 * 1024 * 1024
FFN_VMEM_LIMIT = 62 * 1024 * 1024


def _params(n_axes, vmem_limit=VMEM_LIMIT):
    return pltpu.CompilerParams(
        dimension_semantics=("arbitrary",) * n_axes,
        vmem_limit_bytes=vmem_limit)


def _const_spec(shape):
    zeros = (0,) * len(shape)
    return pl.BlockSpec(shape, lambda *_: zeros, pipeline_mode=pl.Buffered(1))


def _rms(x, g):
    ms = jnp.mean(x * x, axis=-1, keepdims=True)
    return x * lax.rsqrt(ms + NORM_EPS) * g


def _dot(a, b):
    return jnp.dot(a, b, preferred_element_type=F32)


def _in_proj_kernel(x_ref, g_ref, w_ref, n_ref, u_ref, cg_ref, u_scr, *, tt):
    q = SSM_Q
    lanes = 128
    n = _rms(x_ref[0], g_ref[...]).astype(BF16)
    n_ref[0] = n
    u = _dot(n, w_ref[:, 0:SSM_WIDTH])
    for cb in range(SSM_WIDTH // lanes):
        u_scr[cb] = u[:, lanes * cb:lanes * (cb + 1)]
    for s in range(q):
        for cb in range(SSM_WIDTH // lanes):
            c0 = SSM_WIDTH * s + lanes * cb
            u_ref[0, :, c0:c0 + lanes] = u_scr[cb, pl.ds(s, tt // q, stride=q), :]
    blk = GATE_BLOCK
    for j in range(CONV_WIDTH // blk):
        v0 = SSM_WIDTH + blk * j
        g0 = SSM_WIDTH + CONV_WIDTH + blk * j
        val = _dot(n, w_ref[:, v0:v0 + blk])
        gate = _dot(n, w_ref[:, g0:g0 + blk])
        cg = val * jax.nn.sigmoid(gate)
        for e in range(blk // lanes):
            s = j * (blk // lanes) + e
            cg_ref[0, pl.ds(s, tt, stride=SUBLANES), :] = cg[:, lanes * e:lanes * (e + 1)]


def _in_proj(x, g, w, tt):
    b, l, _ = x.shape
    ncol = SSM_WIDTH + 2 * CONV_WIDTH
    q = SSM_Q
    return pl.pallas_call(
        functools.partial(_in_proj_kernel, tt=tt),
        grid=(b, l // tt),
        in_specs=[
            pl.BlockSpec((1, tt, D_MODEL), lambda bi, i: (bi, i, 0)),
            _const_spec((1, D_MODEL)),
            _const_spec((D_MODEL, ncol)),
        ],
        out_specs=[
            pl.BlockSpec((1, tt, D_MODEL), lambda bi, i: (bi, i, 0)),
            pl.BlockSpec((1, tt // q, q * SSM_WIDTH), lambda bi, i: (bi, i, 0)),
            pl.BlockSpec((1, tt * SUBLANES, CONV_LANES), lambda bi, i: (bi, i, 0)),
        ],
        out_shape=[
            jax.ShapeDtypeStruct((b, l, D_MODEL), BF16),
            jax.ShapeDtypeStruct((b, l // q, q * SSM_WIDTH), F32),
            jax.ShapeDtypeStruct((b, l * SUBLANES, CONV_LANES), F32),
        ],
        scratch_shapes=[pltpu.VMEM((SSM_WIDTH // 128, tt, 128), F32)],
        compiler_params=_params(2),
    )(x, g, w)


def _ssm_kernel(u_ref, init_ref, tab_ref, wb_ref, wcr_ref, wci_ref, wk_ref, d_ref,
                y_ref, state_out_ref, x_scr, state_scr, *, m, last_row):
    q = SSM_Q
    lane_blk = 128
    tile_n = 256

    @pl.when(pl.program_id(1) == 0)
    def _():
        state_scr[...] = init_ref[...]

    u = u_ref[0]
    ub = u.astype(BF16)

    for n in range(2 * N_STATE // tile_n):
        kb = (n % (N_STATE // tile_n)) // 2
        lhs = jnp.concatenate(
            [ub[:, SSM_WIDTH * s + lane_blk * kb:SSM_WIDTH * s + lane_blk * (kb + 1)]
             for s in range(q)], axis=1)
        x_scr[:, tile_n * n:tile_n * (n + 1)] = _dot(lhs, wb_ref[n])

    w = 512
    row_id = lax.broadcasted_iota(jnp.int32, (SUBLANES, w), 0)
    for j in range(N_STATE // w):
        sr = slice(w * j, w * (j + 1))
        si = slice(N_STATE + w * j, N_STATE + w * (j + 1))
        tabs = [tab_ref[t, :, sr] for t in range(8)]

        cr, ci = state_scr[:, sr], state_scr[:, si]
        for rb in range(m // SUBLANES):
            rows = slice(rb * SUBLANES, (rb + 1) * SUBLANES)
            xr = x_scr[rows, sr]
            xi = x_scr[rows, si]
            for lvl, d in enumerate((1, 2, 4)):
                ar, ai = tabs[2 * lvl], tabs[2 * lvl + 1]
                pr = pltpu.roll(xr, d, 0)
                pi = pltpu.roll(xi, d, 0)
                xr, xi = xr + ar * pr - ai * pi, xi + ar * pi + ai * pr
            ar, ai = tabs[6], tabs[7]
            xr, xi = xr + ar * cr - ai * ci, xi + ar * ci + ai * cr
            x_scr[rows, sr] = jnp.where(row_id == 0, cr, pltpu.roll(xr, 1, 0))
            x_scr[rows, si] = jnp.where(row_id == 0, ci, pltpu.roll(xi, 1, 0))
            if rb == m // SUBLANES - 1:
                state_scr[:, sr] = jnp.broadcast_to(xr[last_row:last_row + 1], (SUBLANES, w))
                state_scr[:, si] = jnp.broadcast_to(xi[last_row:last_row + 1], (SUBLANES, w))
            cr = jnp.broadcast_to(xr[SUBLANES - 1:SUBLANES], (SUBLANES, w))
            ci = jnp.broadcast_to(xi[SUBLANES - 1:SUBLANES], (SUBLANES, w))

    state_out_ref[0] = state_scr[...]

    half = 256
    st_per_half = half // SSM_GROUP * SSM_STATE
    for h in range(SSM_WIDTH // half):
        er = x_scr[:, st_per_half * h:st_per_half * (h + 1)].astype(BF16)
        ei = x_scr[:, N_STATE + st_per_half * h:N_STATE + st_per_half * (h + 1)].astype(BF16)
        for r in range(q):
            lanes = slice(SSM_WIDTH * r + half * h, SSM_WIDTH * r + half * (h + 1))
            lhs = jnp.concatenate(
                [ub[:, SSM_WIDTH * s + half * h:SSM_WIDTH * s + half * (h + 1)]
                 for s in range(r + 1)], axis=1)
            acc = _dot(er, wcr_ref[r, h]) + _dot(ei, wci_ref[r, h])
            acc = acc + _dot(lhs, wk_ref[r, h, 0:half * (r + 1), :])
            y = jax.nn.gelu(acc + d_ref[:, lanes] * u[:, lanes])
            for e in range(half // lane_blk):
                y_ref[0, (half // lane_blk) * h + e, pl.ds(r, m, stride=q), :] = (
                    y[:, lane_blk * e:lane_blk * (e + 1)])


def _ssm(u4, init_state, prep, m, last_row):
    b, rows, width = u4.shape
    q = SSM_Q
    kern = functools.partial(_ssm_kernel, m=m, last_row=last_row)
    return pl.pallas_call(
        kern,
        grid=(b, rows // m),
        in_specs=[
            pl.BlockSpec((1, m, width), lambda bi, i: (bi, i, 0)),
            _const_spec((SUBLANES, 2 * N_STATE)),
            _const_spec((8, SUBLANES, N_STATE)),
            _const_spec((16, q * 128, 256)),
            _const_spec((q, 2, 1024, 256)),
            _const_spec((q, 2, 1024, 256)),
            _const_spec((q, 2, q * 256, 256)),
            _const_spec((1, width)),
        ],
        out_specs=[
            pl.BlockSpec((1, SSM_WIDTH // 128, q * m, 128), lambda bi, i: (bi, 0, i, 0)),
            pl.BlockSpec((1, SUBLANES, 2 * N_STATE), lambda bi, i: (bi, 0, 0)),
        ],
        out_shape=[
            jax.ShapeDtypeStruct((b, SSM_WIDTH // 128, q * rows, 128), F32),
            jax.ShapeDtypeStruct((b, SUBLANES, 2 * N_STATE), F32),
        ],
        scratch_shapes=[
            pltpu.VMEM((m, 2 * N_STATE), F32),
            pltpu.VMEM((SUBLANES, 2 * N_STATE), F32),
        ],
        compiler_params=_params(2),
    )(u4, init_state, prep["tab"], prep["wb"], prep["wcr"], prep["wci"], prep["wk"], prep["d"])


def _ssm_prep(lam_re, lam_im, log_step, b_re, b_im, c_re, c_im, d_skip):
    q = SSM_Q
    hp = lax.Precision.HIGHEST
    lr = lam_re.astype(F32)
    li = lam_im.astype(F32)
    step = jnp.exp(log_step.astype(F32))[:, None]
    mag = jnp.exp(lr * step)
    ar = mag * jnp.cos(li * step)
    ai = mag * jnp.sin(li * step)
    den = lr * lr + li * li
    cr = ((ar - 1.0) * lr + ai * li) / den
    ci = (ai * lr - (ar - 1.0) * li) / den
    br_ = b_re.astype(F32)
    bi_ = b_im.astype(F32)
    bbr = cr[..., None] * br_ - ci[..., None] * bi_
    bbi = cr[..., None] * bi_ + ci[..., None] * br_

    def powers(xr, xi, n):
        outr, outi = [jnp.ones_like(xr)], [jnp.zeros_like(xi)]
        for _ in range(n):
            pr, pi = outr[-1], outi[-1]
            outr.append(pr * xr - pi * xi)
            outi.append(pr * xi + pi * xr)
        return outr, outi

    pr, pi = powers(ar, ai, q)

    vre = jnp.stack([pr[q - 1 - s][..., None] * bbr - pi[q - 1 - s][..., None] * bbi for s in range(q)])
    vim = jnp.stack([pr[q - 1 - s][..., None] * bbi + pi[q - 1 - s][..., None] * bbr for s in range(q)])
    val = jnp.stack([vre, vim])
    val = val.reshape(2, q, 8, 4, SSM_STATE, SSM_GROUP).transpose(0, 2, 1, 5, 3, 4)
    val = val.reshape(2, 8, q * SSM_GROUP, 256)
    rep_rows = jnp.kron(jnp.eye(q, dtype=F32), jnp.tile(jnp.eye(SSM_GROUP, dtype=F32), (8, 1)))
    wb = jnp.einsum("Rk,anke->anRe", rep_rows, val, precision=hp)
    wb_shape = (8, q * 128, 256)
    row_gl = (lax.broadcasted_iota(jnp.int32, wb_shape, 1) // SSM_GROUP) % 8
    col_gq = lax.broadcasted_iota(jnp.int32, wb_shape, 2) // SSM_STATE
    tile_nb = lax.broadcasted_iota(jnp.int32, wb_shape, 0)
    wb = jnp.where(row_gl == 4 * (tile_nb % 2) + col_gq, wb, 0.0)
    wb = wb.reshape(16, q * 128, 256).astype(BF16)

    cre = c_re.astype(F32)
    cim = c_im.astype(F32)
    care = jnp.stack([cre * pr[t][:, None, :] - cim * pi[t][:, None, :] for t in range(q + 1)])
    caim = jnp.stack([cre * pi[t][:, None, :] + cim * pr[t][:, None, :] for t in range(q + 1)])
    rep_cols = jnp.tile(jnp.eye(SSM_GROUP, dtype=F32), (1, 16))

    def block_diag(v, per_group):
        out = jnp.einsum("rhkc,cn->rhkn", v, rep_cols, precision=hp)
        row_g = lax.broadcasted_iota(jnp.int32, out.shape, 2) // per_group
        col_g = lax.broadcasted_iota(jnp.int32, out.shape, 3) // SSM_GROUP
        return jnp.where(row_g == col_g, out, 0.0)

    def pack_c(cv):
        cv = cv.reshape(q, 2, 16, SSM_GROUP, SSM_STATE).transpose(0, 1, 2, 4, 3)
        return block_diag(cv.reshape(q, 2, 16 * SSM_STATE, SSM_GROUP), SSM_STATE).astype(BF16)

    wcr = pack_c(care[1:])
    wci = pack_c(-caim[1:])

    kt = (jnp.einsum("tgcp,gpd->tgcd", care[:q], bbr, precision=hp)
          - jnp.einsum("tgcp,gpd->tgcd", caim[:q], bbi, precision=hp))
    kt = kt.reshape(q, 2, 16, SSM_GROUP, SSM_GROUP).transpose(0, 1, 2, 4, 3)
    tt = block_diag(kt.reshape(q, 2, 16 * SSM_GROUP, SSM_GROUP), SSM_GROUP)
    zero = jnp.zeros((2, 256, 256), F32)
    wk = jnp.stack([
        jnp.concatenate([tt[r - s] if s <= r else zero for s in range(q)], axis=1)
        for r in range(q)]).astype(BF16)

    aqr, aqi = powers(pr[q].reshape(1, N_STATE), pi[q].reshape(1, N_STATE), SUBLANES)
    rows = jnp.arange(SUBLANES)[:, None]
    tabs = []
    for d in (1, 2, 4):
        keep = (rows >= d).astype(F32)
        tabs += [keep * aqr[d], keep * aqi[d]]
    tabs += [jnp.concatenate(aqr[1:], axis=0), jnp.concatenate(aqi[1:], axis=0)]
    tab = jnp.stack(tabs)

    d4 = jnp.tile(d_skip.astype(F32).reshape(1, SSM_WIDTH), (1, q))
    return dict(wb=wb, wcr=wcr, wci=wci, wk=wk, tab=tab, d=d4)


def _mix_kernel(n_ref, y_ref, cg_ref, hist_ref, *rest, tt):
    n_gate = 2 * D_MODEL // GATE_BLOCK
    wg_refs = rest[:n_gate]
    (wglu_ref, wsp_ref, dww_ref, dwb_ref, lng_ref, lnb_ref, wcp_ref,
     merged_ref, hist_out_ref, buf, conv_scr, cb_scr, gate_scr) = rest[n_gate:]
    hist_rows = CONV_HIST * SUBLANES

    @pl.when(pl.program_id(1) == 0)
    def _():
        buf[0:hist_rows, :] = hist_ref[...]

    @pl.when(pl.program_id(1) != 0)
    def _():
        buf[0:hist_rows, :] = buf[tt * SUBLANES:tt * SUBLANES + hist_rows, :]

    buf[hist_rows:hist_rows + tt * SUBLANES, :] = cg_ref[0]
    hist_out_ref[0] = buf[tt * SUBLANES:tt * SUBLANES + hist_rows, :]

    tb = 16
    off0 = CONV_HIST - (CONV_K - 1)
    blk = MIX_BLOCK
    n = n_ref[0]

    def zero_of(v):
        u = pltpu.bitcast(v, jnp.uint32)
        return pltpu.bitcast((u >> 16) >> 16, F32)

    def plus(a, b):
        return b if a is None else (a if b is None else a + b)

    def conv_block(base, after):
        bias = dwb_ref[...] if after is None else dwb_ref[...] + after
        for sub in range(base, base + tb, SUBLANES):
            accs = [bias for _ in range(SUBLANES)]
            for k in range(CONV_K):
                wk = dww_ref[k]
                for t in range(SUBLANES):
                    r0 = (sub + t + off0 + k) * SUBLANES
                    accs[t] = accs[t] + buf[r0:r0 + SUBLANES, :] * wk
            for t in range(SUBLANES):
                r0 = (sub + t) * SUBLANES
                conv_scr[r0:r0 + SUBLANES, :] = accs[t]

        c = jnp.concatenate(
            [conv_scr[pl.ds(base * SUBLANES + s, tb, stride=SUBLANES), :] for s in range(SUBLANES)],
            axis=1)
        mu = jnp.mean(c, axis=-1, keepdims=True)
        cen = c - mu
        var = jnp.mean(cen * cen, axis=-1, keepdims=True)
        ln = cen * lax.rsqrt(var + LN_EPS) * lng_ref[...] + lnb_ref[...]
        cb_scr[base:base + tb, :] = (ln * jax.nn.sigmoid(ln)).astype(BF16)
        return zero_of(ln[tb - SUBLANES:tb, CONV_WIDTH - 128:CONV_WIDTH])

    n_conv = tt // tb
    gate_done = [None] * n_gate
    conv_done = [None] * n_gate
    for c in range(n_gate):
        cols = slice(GATE_BLOCK * c, GATE_BLOCK * (c + 1))
        after = conv_done[c - 2] if c >= 2 else None
        if after is None:
            lhs = n
        else:
            lead = n[0:16, 0:128] + jnp.concatenate([after, after], axis=0).astype(BF16)
            lhs = jnp.concatenate([lead, n[0:16, 128:]], axis=1)
            if tt > 16:
                lhs = jnp.concatenate([lhs, n[16:]], axis=0)
        g = _dot(lhs, wg_refs[c][...])
        gate_scr[:, cols] = jax.nn.sigmoid(g)
        gate_done[c] = zero_of(g[tt - SUBLANES:tt, GATE_BLOCK - 128:GATE_BLOCK])
        after = plus(gate_done[c - 1], conv_done[c - 1]) if c >= 1 else None
        done = None
        for cbi in range(c * n_conv // n_gate, (c + 1) * n_conv // n_gate):
            done = plus(done, conv_block(cbi * tb, after))
        conv_done[c] = done if done is not None or c == 0 else conv_done[c - 1]

    y = jnp.concatenate([y_ref[0, cbk] for cbk in range(SSM_WIDTH // 128)], axis=1)
    yg = (y * jax.nn.sigmoid(_dot(y.astype(BF16), wglu_ref[...]))).astype(BF16)
    cb = cb_scr[...]
    for j in range(D_MODEL // blk):
        cols = slice(blk * j, blk * (j + 1))
        gcols = slice(D_MODEL + blk * j, D_MODEL + blk * (j + 1))
        ba = _dot(yg, wsp_ref[:, cols])
        bb = _dot(cb, wcp_ref[:, cols])
        merged_ref[0, :, cols] = (gate_scr[:, cols] * ba + gate_scr[:, gcols] * bb).astype(BF16)


def _mix(n, y, cg, hist, w_in, wglu, wsp, dww, dwb, lng, lnb, wcp, tt):
    b, l, _ = n.shape
    gate_blk0 = (SSM_WIDTH + 2 * CONV_WIDTH) // GATE_BLOCK
    kern = functools.partial(_mix_kernel, tt=tt)
    tok = lambda width: pl.BlockSpec((1, tt, width), lambda bi, i: (bi, i, 0))
    return pl.pallas_call(
        kern,
        grid=(b, l // tt),
        in_specs=[
            tok(D_MODEL),
            pl.BlockSpec((1, SSM_WIDTH // 128, tt, 128), lambda bi, i: (bi, 0, i, 0)),
            pl.BlockSpec((1, tt * SUBLANES, CONV_LANES), lambda bi, i: (bi, i, 0)),
            _const_spec((CONV_HIST * SUBLANES, CONV_LANES)),
            *[pl.BlockSpec((D_MODEL, GATE_BLOCK), lambda bi, i, k=k: (0, gate_blk0 + k),
                           pipeline_mode=pl.Buffered(1))
              for k in range(2 * D_MODEL // GATE_BLOCK)],
            _const_spec((SSM_WIDTH, SSM_WIDTH)),
            _const_spec((SSM_WIDTH, D_MODEL)),
            _const_spec((CONV_K, SUBLANES, CONV_LANES)),
            _const_spec((SUBLANES, CONV_LANES)),
            _const_spec((1, CONV_WIDTH)),
            _const_spec((1, CONV_WIDTH)),
            _const_spec((CONV_WIDTH, D_MODEL)),
        ],
        out_specs=[
            tok(D_MODEL),
            pl.BlockSpec((1, CONV_HIST * SUBLANES, CONV_LANES), lambda bi, i: (bi, 0, 0)),
        ],
        out_shape=[
            jax.ShapeDtypeStruct((b, l, D_MODEL), BF16),
            jax.ShapeDtypeStruct((b, CONV_HIST * SUBLANES, CONV_LANES), F32),
        ],
        scratch_shapes=[
            pltpu.VMEM(((CONV_HIST + tt) * SUBLANES, CONV_LANES), F32),
            pltpu.VMEM((tt * SUBLANES, CONV_LANES), F32),
            pltpu.VMEM((tt, CONV_WIDTH), BF16),
            pltpu.VMEM((tt, 2 * D_MODEL), F32),
        ],
        compiler_params=_params(2, MIX_VMEM_LIMIT),
    )(n, y, cg, hist, *([w_in] * (2 * D_MODEL // GATE_BLOCK)), wglu, wsp,
      dww.reshape(CONV_K, SUBLANES, CONV_LANES), dwb.reshape(SUBLANES, CONV_LANES), lng, lnb, wcp)


def _mix_out_kernel(x_ref, m_ref, w_ref, g_ref, g2_ref, h_ref, n2_ref):
    o = _dot(m_ref[0], w_ref[...])
    h = x_ref[0] + _rms(o, g_ref[...])
    h_ref[0] = h
    n2_ref[0] = _rms(h, g2_ref[...]).astype(BF16)


def _mix_out(x, merged, w, g, g2, tt):
    b, l, _ = x.shape
    tok = pl.BlockSpec((1, tt, D_MODEL), lambda bi, i: (bi, i, 0))
    return pl.pallas_call(
        _mix_out_kernel,
        grid=(b, l // tt),
        in_specs=[tok, tok, _const_spec((D_MODEL, D_MODEL)), _const_spec((1, D_MODEL)),
                  _const_spec((1, D_MODEL))],
        out_specs=[tok, tok],
        out_shape=[jax.ShapeDtypeStruct((b, l, D_MODEL), F32),
                   jax.ShapeDtypeStruct((b, l, D_MODEL), BF16)],
        compiler_params=_params(2),
    )(x, merged, w, g, g2)


def _ffn_kernel(h_hbm, n_ref, hist_ref, wg_ref, wv_ref, dw_ref, b_ref,
                wd_ref, gpost_ref, out_ref, hist_out_ref, ubuf, act_scr, carry, h_buf, h_sem,
                *, tt, nf):
    bi = pl.program_id(0)
    i = pl.program_id(1)
    j = pl.program_id(2)
    fb = FFN_BLOCK

    def residual_copy():
        return pltpu.make_async_copy(h_hbm.at[bi, pl.ds(i * tt, tt), :], h_buf, h_sem)

    @pl.when(j == 0)
    def _():
        residual_copy().start()
        out_ref[0] = jnp.zeros((tt, D_MODEL), F32)

    @pl.when(i == 0)
    def _():
        carry[j] = hist_ref[j]

    ubuf[0:FFN_HIST, :] = carry[j]
    dwg, dwv = dw_ref[j], dw_ref[nf + j]
    taps = [jnp.concatenate([dwg[k:k + 1, :], dwv[k:k + 1, :]], axis=1) for k in range(FFN_K)]
    bias = jnp.concatenate([b_ref[j], b_ref[nf + j]], axis=1)

    slab = min(tt, FFN_SLAB)

    def up(r0):
        n = n_ref[0, r0:r0 + slab, :]
        ubuf[FFN_HIST + r0:FFN_HIST + r0 + slab, 0:fb] = _dot(n, wg_ref[...])
        ubuf[FFN_HIST + r0:FFN_HIST + r0 + slab, fb:2 * fb] = _dot(n, wv_ref[...])

    up(0)
    for r0 in range(0, tt, slab):
        if r0 + slab < tt:
            up(r0 + slab)
        for c0 in range(r0, r0 + slab, FFN_CHUNK):
            win = ubuf[c0:c0 + FFN_HIST + FFN_CHUNK, :]
            conv = bias + win[FFN_HIST:] * taps[FFN_K - 1]
            for k in range(FFN_K - 1):
                d = FFN_K - 1 - k
                conv = conv + pltpu.roll(win, d, 0)[FFN_HIST:] * taps[k]
            act_scr[c0:c0 + FFN_CHUNK, :] = (
                jax.nn.gelu(conv[:, 0:fb]) * conv[:, fb:2 * fb]).astype(BF16)
        out_ref[0, r0:r0 + slab, :] += _dot(act_scr[r0:r0 + slab, :], wd_ref[...])

    tail = ubuf[tt:tt + FFN_HIST, :]
    carry[j] = tail
    hist_out_ref[0, j] = tail

    @pl.when(j == nf - 1)
    def _():
        residual_copy().wait()
        for c0 in range(0, tt, FFN_CHUNK):
            rows = slice(c0, c0 + FFN_CHUNK)
            out_ref[0, rows, :] = h_buf[rows, :] + _rms(out_ref[0, rows, :], gpost_ref[...])


def _ffn(h, n2, hist, wup, dww, dwb, wdown, gpost, tt):
    b, l, _ = h.shape
    fb = FFN_BLOCK
    nf = D_FF // fb
    kern = functools.partial(_ffn_kernel, tt=tt, nf=nf)
    tok = pl.BlockSpec((1, tt, D_MODEL), lambda bi, i, j: (bi, i, 0))
    return pl.pallas_call(
        kern,
        grid=(b, l // tt, nf),
        in_specs=[
            pl.BlockSpec(memory_space=pl.ANY),
            tok,
            _const_spec((nf, FFN_HIST, 2 * fb)),
            pl.BlockSpec((D_MODEL, fb), lambda bi, i, j: (0, j)),
            pl.BlockSpec((D_MODEL, fb), lambda bi, i, j: (0, nf + j)),
            _const_spec((2 * nf, FFN_K, fb)),
            _const_spec((2 * nf, 1, fb)),
            pl.BlockSpec((fb, D_MODEL), lambda bi, i, j: (j, 0)),
            _const_spec((1, D_MODEL)),
        ],
        out_specs=[
            tok,
            pl.BlockSpec((1, nf, FFN_HIST, 2 * fb), lambda bi, i, j: (bi, 0, 0, 0)),
        ],
        out_shape=[
            jax.ShapeDtypeStruct((b, l, D_MODEL), F32),
            jax.ShapeDtypeStruct((b, nf, FFN_HIST, 2 * fb), F32),
        ],
        scratch_shapes=[
            pltpu.VMEM((FFN_HIST + tt, 2 * fb), F32),
            pltpu.VMEM((tt, fb), BF16),
            pltpu.VMEM((nf, FFN_HIST, 2 * fb), F32),
            pltpu.VMEM((tt, D_MODEL), F32),
            pltpu.SemaphoreType.DMA(()),
        ],
        compiler_params=_params(3, FFN_VMEM_LIMIT),
    )(h, n2, hist, wup, wup, dww, dwb, wdown, gpost)


def _block(x, carries, p, ssm, tiles):
    b, l, _ = x.shape
    q = SSM_Q
    tt_in, m_ssm, tt_mix, tt_out, tt_ffn = tiles
    state0, conv_hist0, ffn_hist0 = carries

    n, u4, cg = _in_proj(x, p["g_mix_pre"], p["w_in"], tt_in)

    rows = l // q
    pad = (-rows) % m_ssm
    if pad:
        u4 = jnp.pad(u4, ((0, 0), (0, pad), (0, 0)))
    last_row = (rows - 1) % SUBLANES
    y, state = _ssm(u4, state0, ssm, m_ssm, last_row)
    if pad:
        y = y[:, :, :l]

    merged, conv_hist = _mix(n, y, cg, conv_hist0, p["w_in"], p["w_glu"],
                             p["w_sp"], p["conv_w"], p["conv_b"], p["ln_g"], p["ln_b"],
                             p["w_cp"], tt_mix)
    h1, n2 = _mix_out(x, merged, p["w_mo"], p["g_mix_post"], p["g_ffn_pre"], tt_out)
    out, ffn_hist = _ffn(h1, n2, ffn_hist0, p["w_up"], p["ffn_w"], p["ffn_b"],
                         p["w_down"], p["g_ffn_post"], tt_ffn)
    return out, (state[0], conv_hist[0], ffn_hist[0])


def kernel(x, meta_tokens, norm_mix_pre, w_in, lam_re, lam_im, log_step, ssm_b_re, ssm_b_im,
           ssm_c_re, ssm_c_im, ssm_d, w_ssm_glu, w_ssm_proj, conv_dw_w, conv_dw_b, conv_ln_g,
           conv_ln_b, w_conv_proj, w_mix_out, norm_mix_post, norm_ffn_pre, w_ffn_up, ffn_dw_w,
           ffn_dw_b, w_ffn_down, norm_ffn_post):
    depth = w_in.shape[0]
    nf = D_FF // FFN_BLOCK
    zero_carries = (
        jnp.zeros((SUBLANES, 2 * N_STATE), F32),
        jnp.zeros((CONV_HIST * SUBLANES, CONV_LANES), F32),
        jnp.zeros((nf, FFN_HIST, 2 * FFN_BLOCK), F32),
    )
    h_meta = meta_tokens.astype(x.dtype)[None]
    h = x
    row = lambda v: v.astype(F32).reshape(1, -1)
    blocked = lambda v: v.reshape(v.shape[0], 2 * nf, FFN_BLOCK).transpose(1, 0, 2)
    for i in range(depth):
        p = dict(
            g_mix_pre=row(norm_mix_pre[i]),
            w_in=w_in[i].astype(BF16),
            w_glu=w_ssm_glu[i].astype(BF16),
            w_sp=w_ssm_proj[i].astype(BF16),
            conv_w=conv_dw_w[i].astype(F32),
            conv_b=row(conv_dw_b[i]),
            ln_g=row(conv_ln_g[i]),
            ln_b=row(conv_ln_b[i]),
            w_cp=w_conv_proj[i].astype(BF16),
            w_mo=w_mix_out[i].astype(BF16),
            g_mix_post=row(norm_mix_post[i]),
            g_ffn_pre=row(norm_ffn_pre[i]),
            w_up=w_ffn_up[i].astype(BF16),
            ffn_w=blocked(ffn_dw_w[i].astype(F32)),
            ffn_b=blocked(row(ffn_dw_b[i])),
            w_down=w_ffn_down[i].astype(BF16),
            g_ffn_post=row(norm_ffn_post[i]),
        )
        ssm = _ssm_prep(lam_re[i], lam_im[i], log_step[i], ssm_b_re[i], ssm_b_im[i],
                        ssm_c_re[i], ssm_c_im[i], ssm_d[i])
        h_meta, carries = _block(h_meta, zero_carries, p, ssm,
                                 (N_META, SUBLANES, N_META, N_META, N_META))
        h, _ = _block(h, carries, p, ssm, (512, 512, 512, 512, 1024))
    return h
```
